```python
import jax
import jax.numpy as jnp
from jax import lax
import numpy as np

D_MODEL = 1024
BATCH = 16
SEQ = 2048
DEPTH = 2

GRID_W = 64
CTX_LEN = 256
N_BRANCH = 4
BRANCH_WIDTH = D_MODEL // 2
CONV_WIDTH = 4
CONV_PAD = (2, 1)
LRU_WIDTH = BRANCH_WIDTH
LRU_BLOCKS = 8
LRU_BLOCK_W = LRU_WIDTH // LRU_BLOCKS
LRU_C = 8.0
ATTN_HEAD_DIM = 64
ATTN_HEADS = BRANCH_WIDTH // ATTN_HEAD_DIM
ATTN_KV_HEADS = 2
ATTN_WINDOW = 128
ATTN_BLOCK = 128
ROPE_BASE = 10000.0
HGRN_HEAD_DIM = 128
HGRN_HEADS = BRANCH_WIDTH // HGRN_HEAD_DIM
HGRN_CHUNK = 64
SSD_WIDTH = BRANCH_WIDTH
SSD_HEAD_DIM = 64
SSD_HEADS = SSD_WIDTH // SSD_HEAD_DIM
SSD_GROUPS = 2
SSD_HPG = SSD_HEADS // SSD_GROUPS
SSD_STATE = 128
SSD_CHUNK = 64
MLP_HIDDEN = 4 * D_MODEL
NORM_EPS = 1e-6

IN_SPLITS = (
    LRU_WIDTH, LRU_WIDTH,
    ATTN_HEADS * ATTN_HEAD_DIM,
    ATTN_KV_HEADS * ATTN_HEAD_DIM, ATTN_KV_HEADS * ATTN_HEAD_DIM,
    BRANCH_WIDTH, BRANCH_WIDTH,
    BRANCH_WIDTH, BRANCH_WIDTH,
    BRANCH_WIDTH,
    SSD_WIDTH,
    SSD_WIDTH + 2 * SSD_GROUPS * SSD_STATE,
    SSD_HEADS, SSD_HEADS,
    N_BRANCH * D_MODEL,
)
IN_COLS = sum(IN_SPLITS)

kernel_name = 'hybrid_lru_swa_hgrn2_ssd_prefix_dit'


def rmsnorm(x, g):
    xf = x.astype(jnp.float32)
    y = xf * lax.rsqrt(jnp.mean(xf * xf, axis=-1, keepdims=True) + NORM_EPS)
    return (y * g.astype(jnp.float32)).astype(x.dtype)


def modulate(h, shift, scale):
    return h * (1 + scale) + shift


def split_columns(p):
    return jnp.split(p, np.cumsum(IN_SPLITS)[:-1].tolist(), axis=-1)


def dwconv(x, w, b):
    y = lax.conv_general_dilated(x, w[:, None, :], window_strides=(1,), padding=[CONV_PAD],
                                 dimension_numbers=('NWC', 'WIO', 'NWC'),
                                 feature_group_count=x.shape[-1])
    return y + b


def to_chunks(t, ch):
    return jnp.swapaxes(t.reshape(t.shape[0], t.shape[1] // ch, ch, *t.shape[2:]), 0, 1)


def from_chunks(t):
    t = jnp.swapaxes(t, 0, 1)
    return t.reshape(t.shape[0], t.shape[1] * t.shape[2], *t.shape[3:])


def axial_rope(x, rows, cols):
    half = x.shape[-1] // 2
    quarter = half // 2
    inv_freq = ROPE_BASE ** (-jnp.arange(quarter, dtype=jnp.float32) / quarter)

    def rotate(xa, pos):
        ang = pos.astype(jnp.float32)[:, None] * inv_freq
        cos = jnp.cos(ang)[None, :, None, :].astype(x.dtype)
        sin = jnp.sin(ang)[None, :, None, :].astype(x.dtype)
        x1, x2 = xa[..., :quarter], xa[..., quarter:]
        return jnp.concatenate([x1 * cos - x2 * sin, x2 * cos + x1 * sin], axis=-1)

    return jnp.concatenate([rotate(x[..., :half], rows), rotate(x[..., half:], cols)], axis=-1)


def linear_scan(a, b, h0, reverse):
    idx = -1 if reverse else 0
    b = b.at[:, idx].add(a[:, idx] * h0)

    def combine(l, r):
        return (l[0] * r[0], r[0] * l[1] + r[1])

    return lax.associative_scan(combine, (a, b), reverse=reverse, axis=1)[1]


def rglru_branch(u, y_gate, u_c, y_gate_c, conv_w, conv_b, rec_w, rec_b, inp_w, inp_b, lam,
                 with_ctx_out):
    x_l = dwconv(u, conv_w, conv_b)
    x_c = dwconv(u_c, conv_w, conv_b)

    def coeffs(xx, d):
        xb = xx.reshape(*xx.shape[:2], LRU_BLOCKS, LRU_BLOCK_W)
        r = jax.nn.sigmoid(jnp.einsum('bthi,hij->bthj', xb, rec_w[d]).reshape(xx.shape) + rec_b[d])
        i = jax.nn.sigmoid(jnp.einsum('bthi,hij->bthj', xb, inp_w[d]).reshape(xx.shape) + inp_b[d])
        log_a = (-LRU_C * r * jax.nn.softplus(-lam[d])).astype(jnp.float32)
        a = jnp.exp(log_a)
        b = jnp.sqrt(-jnp.expm1(2.0 * log_a)) * (i * xx).astype(jnp.float32)
        return a, b

    h0 = jnp.zeros((u_c.shape[0], LRU_WIDTH), jnp.float32)
    h_lat, h_ctx = 0.0, 0.0
    for d, rev in ((0, False), (1, True)):
        a_c, b_c = coeffs(x_c, d)
        hc = linear_scan(a_c, b_c, h0, rev)
        a_l, b_l = coeffs(x_l, d)
        hl = linear_scan(a_l, b_l, hc[:, 0] if rev else hc[:, -1], rev)
        h_lat = h_lat + hl
        h_ctx = h_ctx + hc
    y_lat = h_lat.astype(u.dtype) * jax.nn.gelu(y_gate)
    y_ctx = h_ctx.astype(u.dtype) * jax.nn.gelu(y_gate_c) if with_ctx_out else None
    return y_lat, y_ctx


def window_attention(q, k, v, k_ctx, v_ctx, sink):
    B_, L, H, hd = q.shape
    G = k.shape[2]
    R = H // G
    nb = L // ATTN_BLOCK
    N = k_ctx.shape[1]
    span = 3 * ATTN_BLOCK
    scale = hd ** -0.5
    qb = q.reshape(B_, nb, ATTN_BLOCK, G, R, hd)

    def band(t):
        tp = jnp.pad(t, ((0, 0), (ATTN_BLOCK, ATTN_BLOCK), (0, 0), (0, 0)))
        tp = tp.reshape(B_, nb + 2, ATTN_BLOCK, G, hd)
        return jnp.concatenate([tp[:, :-2], tp[:, 1:-1], tp[:, 2:]], axis=2)

    kb, vb = band(k), band(v)
    s_loc = jnp.einsum('bnqgrd,bnkgd->bngrqk', qb, kb).astype(jnp.float32) * scale
    q_pos = jnp.arange(nb)[:, None] * ATTN_BLOCK + jnp.arange(ATTN_BLOCK)[None, :]
    k_pos = (jnp.arange(nb)[:, None] - 1) * ATTN_BLOCK + jnp.arange(span)[None, :]
    valid = ((jnp.abs(q_pos[:, :, None] - k_pos[:, None, :]) <= ATTN_WINDOW)
             & (k_pos[:, None, :] >= 0) & (k_pos[:, None, :] < L))
    s_loc = jnp.where(valid[None, :, None, None], s_loc, -jnp.inf)
    s_ctx = jnp.einsum('bnqgrd,bcgd->bngrqc', qb, k_ctx).astype(jnp.float32) * scale
    s_sink = jnp.broadcast_to(sink.astype(jnp.float32).reshape(1, 1, G, R, 1, 1),
                              (B_, nb, G, R, ATTN_BLOCK, 1))
    p = jax.nn.softmax(jnp.concatenate([s_loc, s_ctx, s_sink], axis=-1), axis=-1).astype(v.dtype)
    o = (jnp.einsum('bngrqk,bnkgd->bnqgrd', p[..., :span], vb)
         + jnp.einsum('bngrqc,bcgd->bnqgrd', p[..., span:span + N], v_ctx))
    return o.reshape(B_, L, H * hd)


def ctx_attention(q, k, v, sink):
    B_, N, H, hd = q.shape
    G = k.shape[2]
    R = H // G
    qg = q.reshape(B_, N, G, R, hd)
    s = jnp.einsum('bqgrd,bkgd->bgrqk', qg, k).astype(jnp.float32) * hd ** -0.5
    s_sink = jnp.broadcast_to(sink.astype(jnp.float32).reshape(1, G, R, 1, 1), (B_, G, R, N, 1))
    p = jax.nn.softmax(jnp.concatenate([s, s_sink], axis=-1), axis=-1)[..., :N].astype(v.dtype)
    return jnp.einsum('bgrqk,bkgd->bqgrd', p, v).reshape(B_, N, H * hd)


def attention_branch(q, k, v, q_c, k_c, v_c, sink, rows, cols, with_ctx_out):
    B_, L = q.shape[:2]
    N = q_c.shape[1]
    qh = axial_rope(q.reshape(B_, L, ATTN_HEADS, ATTN_HEAD_DIM), rows, cols)
    kh = axial_rope(k.reshape(B_, L, ATTN_KV_HEADS, ATTN_HEAD_DIM), rows, cols)
    vh = v.reshape(B_, L, ATTN_KV_HEADS, ATTN_HEAD_DIM)
    kch = k_c.reshape(B_, N, ATTN_KV_HEADS, ATTN_HEAD_DIM)
    vch = v_c.reshape(B_, N, ATTN_KV_HEADS, ATTN_HEAD_DIM)
    y_lat = window_attention(qh, kh, vh, kch, vch, sink)
    y_ctx = (ctx_attention(q_c.reshape(B_, N, ATTN_HEADS, ATTN_HEAD_DIM), kch, vch, sink)
             if with_ctx_out else None)
    return y_lat, y_ctx


def gla_chunk_scan(q, k, v, log_f, s0, reverse):
    seqs = [t.astype(jnp.float32) for t in (q, k, v, log_f)]
    if reverse:
        seqs = [jnp.flip(t, axis=1) for t in seqs]
    tri = jnp.tril(jnp.ones((HGRN_CHUNK, HGRN_CHUNK), bool))[None, :, :, None, None]

    def step(state, blk):
        qc, kc, vc, lfc = blk
        cum = jnp.cumsum(lfc, axis=1)
        decay = jnp.exp(jnp.where(tri, cum[:, :, None] - cum[:, None, :], -jnp.inf))
        scores = jnp.einsum('bthk,btshk,bshk->bths', qc, decay, kc)
        out = (jnp.einsum('bths,bshv->bthv', scores, vc)
               + jnp.einsum('bthk,bhkv->bthv', qc * jnp.exp(cum), state))
        total = cum[:, -1:]
        state = (state * jnp.exp(total[:, 0])[..., None]
                 + jnp.einsum('bshk,bshv->bhkv', kc * jnp.exp(total - cum), vc))
        return state, out

    s_end, out = lax.scan(step, s0, tuple(to_chunks(t, HGRN_CHUNK) for t in seqs))
    out = from_chunks(out)
    if reverse:
        out = jnp.flip(out, axis=1)
    return out, s_end


def hgrn2_branch(q, v, f_fwd, f_bwd, g, q_c, v_c, f_fwd_c, f_bwd_c, g_c, lb, norm_g, with_ctx_out):
    def heads(t):
        return t.reshape(*t.shape[:2], HGRN_HEADS, HGRN_HEAD_DIM)

    lb = lb.astype(jnp.float32).reshape(HGRN_HEADS, HGRN_HEAD_DIM)
    log_lb, log_rest = jnp.log(lb), jnp.log1p(-lb)

    def forget(z):
        lf = jnp.logaddexp(log_lb, log_rest + jax.nn.log_sigmoid(heads(z).astype(jnp.float32)))
        return lf, -jnp.expm1(lf)

    ql, vl = heads(jax.nn.silu(q)), heads(v)
    qc, vc = heads(jax.nn.silu(q_c)), heads(v_c)
    s0 = jnp.zeros((q_c.shape[0], HGRN_HEADS, HGRN_HEAD_DIM, HGRN_HEAD_DIM), jnp.float32)
    o_lat, o_ctx = 0.0, 0.0
    for z_l, z_c, rev in ((f_fwd, f_fwd_c, False), (f_bwd, f_bwd_c, True)):
        lf_c, k_c = forget(z_c)
        oc, s_ctx_end = gla_chunk_scan(qc, k_c, vc, lf_c, s0, rev)
        lf_l, k_l = forget(z_l)
        ol, _ = gla_chunk_scan(ql, k_l, vl, lf_l, s_ctx_end, rev)
        o_lat = o_lat + ol
        o_ctx = o_ctx + oc
    gain = norm_g.reshape(HGRN_HEADS, HGRN_HEAD_DIM)

    def readout(o, gate):
        return (rmsnorm(o.astype(gate.dtype), gain) * jax.nn.silu(heads(gate))).reshape(gate.shape)

    return readout(o_lat, g), (readout(o_ctx, g_c) if with_ctx_out else None)


def ssd_chunk_scan(x, dt, bm, cm, a, s0, reverse):
    seqs = [t.astype(jnp.float32) for t in (x, dt, bm, cm)]
    if reverse:
        seqs = [jnp.flip(t, axis=1) for t in seqs]
    tri = jnp.tril(jnp.ones((SSD_CHUNK, SSD_CHUNK), bool))[None, :, :, None, None]

    def step(state, blk):
        xc, dtc, bc, cc = blk
        cum = jnp.cumsum(dtc * a, axis=1)
        decay = jnp.exp(jnp.where(tri, cum[:, :, None] - cum[:, None, :], -jnp.inf))
        mix = jnp.einsum('btgn,bsgn->btsg', cc, bc)[..., None] * decay * dtc[:, None]
        out = (jnp.einsum('btsgr,bsgrp->btgrp', mix, xc)
               + jnp.einsum('btgn,bgrnp->btgrp', cc, state) * jnp.exp(cum)[..., None])
        total = cum[:, -1:]
        w = jnp.exp(total - cum) * dtc
        state = (state * jnp.exp(total[:, 0])[..., None, None]
                 + jnp.einsum('bsgn,bsgr,bsgrp->bgrnp', bc, w, xc))
        return state, out

    s_end, out = lax.scan(step, s0, tuple(to_chunks(t, SSD_CHUNK) for t in seqs))
    out = from_chunks(out)
    if reverse:
        out = jnp.flip(out, axis=1)
    return out, s_end


def ssd_branch(z, xbc, dt_f, dt_b, z_c, xbc_c, dt_f_c, dt_b_c, conv_w, conv_b, dt_bias, a_log,
               skip, norm_g, with_ctx_out):
    def prep(xbc_raw):
        u = jax.nn.silu(dwconv(xbc_raw, conv_w, conv_b))
        xs, bm, cm = jnp.split(u, [SSD_WIDTH, SSD_WIDTH + SSD_GROUPS * SSD_STATE], axis=-1)
        B_, T = u.shape[:2]
        return (xs.reshape(B_, T, SSD_GROUPS, SSD_HPG, SSD_HEAD_DIM),
                bm.reshape(B_, T, SSD_GROUPS, SSD_STATE), cm.reshape(B_, T, SSD_GROUPS, SSD_STATE))

    x_l, b_l, c_l = prep(xbc)
    x_c, b_c, c_c = prep(xbc_c)
    s0 = jnp.zeros((x_c.shape[0], SSD_GROUPS, SSD_HPG, SSD_STATE, SSD_HEAD_DIM), jnp.float32)
    d_skip = skip.reshape(SSD_GROUPS, SSD_HPG)[..., None]
    y_lat = d_skip * x_l
    y_ctx = d_skip * x_c
    for d, (dl, dc, rev) in enumerate(((dt_f, dt_f_c, False), (dt_b, dt_b_c, True))):
        a = -jnp.exp(a_log[d].astype(jnp.float32)).reshape(SSD_GROUPS, SSD_HPG)
        dtl = jax.nn.softplus(dl + dt_bias[d]).reshape(*dl.shape[:2], SSD_GROUPS, SSD_HPG)
        dtc = jax.nn.softplus(dc + dt_bias[d]).reshape(*dc.shape[:2], SSD_GROUPS, SSD_HPG)
        yc, s_ctx_end = ssd_chunk_scan(x_c, dtc, b_c, c_c, a, s0, rev)
        yl, _ = ssd_chunk_scan(x_l, dtl, b_l, c_l, a, s_ctx_end, rev)
        y_lat = y_lat + yl.astype(x_l.dtype)
        y_ctx = y_ctx + yc.astype(x_c.dtype)
    gain = norm_g.reshape(SSD_GROUPS, SSD_WIDTH // SSD_GROUPS)

    def readout(y, gate):
        B_, T = gate.shape[:2]
        yg = (y.reshape(B_, T, SSD_WIDTH) * jax.nn.silu(gate)).reshape(B_, T, SSD_GROUPS, -1)
        return rmsnorm(yg, gain).reshape(B_, T, SSD_WIDTH)

    return readout(y_lat, z), (readout(y_ctx, z_c) if with_ctx_out else None)


def merge_branches(ys, gate_logits, w_branch, w_out):
    gates = jax.nn.sigmoid(gate_logits.reshape(*gate_logits.shape[:-1], N_BRANCH, D_MODEL))
    merged = gates[..., 0, :] * (ys[0] @ w_branch[0])
    for i in range(1, N_BRANCH):
        merged = merged + gates[..., i, :] * (ys[i] @ w_branch[i])
    return merged @ w_out


def hybrid_mixer(h_lat, h_ctx, rows, cols, w_in, lru_conv_w, lru_conv_b, lru_rec_w, lru_rec_b,
                 lru_inp_w, lru_inp_b, lru_lambda, attn_sink, hgrn_lb, hgrn_norm_g, ssd_conv_w,
                 ssd_conv_b, ssd_dt_bias, ssd_a_log, ssd_skip, ssd_norm_g, w_branch, w_out,
                 with_ctx_out):
    (l_lx, l_lg, l_q, l_k, l_v, l_hq, l_hi, l_hff, l_hfb, l_hg, l_z, l_xbc, l_dtf, l_dtb,
     l_mg) = split_columns(h_lat @ w_in)
    (c_lx, c_lg, c_q, c_k, c_v, c_hq, c_hi, c_hff, c_hfb, c_hg, c_z, c_xbc, c_dtf, c_dtb,
     c_mg) = split_columns(h_ctx @ w_in)
    ya = rglru_branch(l_lx, l_lg, c_lx, c_lg, lru_conv_w, lru_conv_b, lru_rec_w, lru_rec_b,
                      lru_inp_w, lru_inp_b, lru_lambda, with_ctx_out)
    yb = attention_branch(l_q, l_k, l_v, c_q, c_k, c_v, attn_sink, rows, cols, with_ctx_out)
    yc = hgrn2_branch(l_hq, l_hi, l_hff, l_hfb, l_hg, c_hq, c_hi, c_hff, c_hfb, c_hg,
                      hgrn_lb, hgrn_norm_g, with_ctx_out)
    yd = ssd_branch(l_z, l_xbc, l_dtf, l_dtb, c_z, c_xbc, c_dtf, c_dtb, ssd_conv_w, ssd_conv_b,
                    ssd_dt_bias, ssd_a_log, ssd_skip, ssd_norm_g, with_ctx_out)
    out_lat = merge_branches([ya[0], yb[0], yc[0], yd[0]], l_mg, w_branch, w_out)
    out_ctx = (merge_branches([ya[1], yb[1], yc[1], yd[1]], c_mg, w_branch, w_out)
               if with_ctx_out else None)
    return out_lat, out_ctx


def sqrelu_mlp(h, w_up, w_down):
    return jnp.square(jax.nn.relu(h @ w_up)) @ w_down


def setup_inputs(seed: int = 0) -> dict:
    key = jax.random.key(seed)
    ks = iter(jax.random.split(key, 40))
    f32 = jnp.float32

    def nrm(shape, s):
        return jax.random.normal(next(ks), shape, f32) * s

    def unif(shape, lo, hi):
        return jax.random.uniform(next(ks), shape, f32, lo, hi)

    def gain(shape):
        return 1.0 + nrm(shape, 0.05)

    L = DEPTH
    x = nrm((BATCH, SEQ, D_MODEL), 1.0)
    c = nrm((BATCH, D_MODEL), 1.0)
    ctx = nrm((BATCH, CTX_LEN, D_MODEL), 1.0)
    c_ctx = nrm((D_MODEL,), 1.0)
    w_ada = nrm((L, D_MODEL, 6 * D_MODEL), 0.5 * D_MODEL ** -0.5)
    b_ada = nrm((L, 6 * D_MODEL), 0.01)
    g_pre_mix = gain((L, D_MODEL))
    g_post_mix = gain((L, D_MODEL))
    g_pre_mlp = gain((L, D_MODEL))
    g_post_mlp = gain((L, D_MODEL))
    w_in = nrm((L, D_MODEL, IN_COLS), D_MODEL ** -0.5)
    lru_conv_w = nrm((L, CONV_WIDTH, LRU_WIDTH), CONV_WIDTH ** -0.5)
    lru_conv_b = nrm((L, LRU_WIDTH), 0.01)
    lru_rec_w = nrm((L, 2, LRU_BLOCKS, LRU_BLOCK_W, LRU_BLOCK_W), LRU_BLOCK_W ** -0.5)
    lru_rec_b = nrm((L, 2, LRU_WIDTH), 0.01)
    lru_inp_w = nrm((L, 2, LRU_BLOCKS, LRU_BLOCK_W, LRU_BLOCK_W), LRU_BLOCK_W ** -0.5)
    lru_inp_b = nrm((L, 2, LRU_WIDTH), 0.01)
    a_init = unif((L, 2, LRU_WIDTH), 0.9, 0.999) ** (1.0 / LRU_C)
    lru_lambda = jnp.log(a_init) - jnp.log1p(-a_init)
    attn_sink = nrm((L, ATTN_HEADS), 0.5)
    hgrn_lb_logits = nrm((L, BRANCH_WIDTH), 1.0)
    hgrn_norm_g = gain((L, BRANCH_WIDTH))
    ssd_conv_w = nrm((L, CONV_WIDTH, SSD_WIDTH + 2 * SSD_GROUPS * SSD_STATE), CONV_WIDTH ** -0.5)
    ssd_conv_b = nrm((L, SSD_WIDTH + 2 * SSD_GROUPS * SSD_STATE), 0.01)
    dt0 = jnp.exp(unif((L, 2, SSD_HEADS), float(np.log(1e-3)), float(np.log(1e-1))))
    ssd_dt_bias = dt0 + jnp.log(-jnp.expm1(-dt0))
    ssd_a_log = jnp.log(unif((L, 2, SSD_HEADS), 1.0, 16.0))
    ssd_skip = gain((L, SSD_HEADS))
    ssd_norm_g = gain((L, SSD_WIDTH))
    w_branch = nrm((L, N_BRANCH, BRANCH_WIDTH, D_MODEL), BRANCH_WIDTH ** -0.5)
    w_out = nrm((L, D_MODEL, D_MODEL), D_MODEL ** -0.5)
    w_mlp_up = nrm((L, D_MODEL, MLP_HIDDEN), D_MODEL ** -0.5)
    w_mlp_down = nrm((L, MLP_HIDDEN, D_MODEL), MLP_HIDDEN ** -0.5)
    return {
        'x': x, 'c': c, 'ctx': ctx, 'c_ctx': c_ctx, 'w_ada': w_ada, 'b_ada': b_ada,
        'g_pre_mix': g_pre_mix, 'g_post_mix': g_post_mix, 'g_pre_mlp': g_pre_mlp,
        'g_post_mlp': g_post_mlp, 'w_in': w_in, 'lru_conv_w': lru_conv_w, 'lru_conv_b': lru_conv_b,
        'lru_rec_w': lru_rec_w, 'lru_rec_b': lru_rec_b, 'lru_inp_w': lru_inp_w,
        'lru_inp_b': lru_inp_b, 'lru_lambda': lru_lambda, 'attn_sink': attn_sink,
        'hgrn_lb_logits': hgrn_lb_logits, 'hgrn_norm_g': hgrn_norm_g, 'ssd_conv_w': ssd_conv_w,
        'ssd_conv_b': ssd_conv_b, 'ssd_dt_bias': ssd_dt_bias, 'ssd_a_log': ssd_a_log,
        'ssd_skip': ssd_skip, 'ssd_norm_g': ssd_norm_g, 'w_branch': w_branch, 'w_out': w_out,
        'w_mlp_up': w_mlp_up, 'w_mlp_down': w_mlp_down,
    }


def reference(x, c, ctx, c_ctx, w_ada, b_ada, g_pre_mix, g_post_mix, g_pre_mlp, g_post_mlp, w_in,
              lru_conv_w, lru_conv_b, lru_rec_w, lru_rec_b, lru_inp_w, lru_inp_b, lru_lambda,
              attn_sink, hgrn_lb_logits, hgrn_norm_g, ssd_conv_w, ssd_conv_b, ssd_dt_bias,
              ssd_a_log, ssd_skip, ssd_norm_g, w_branch, w_out, w_mlp_up, w_mlp_down):
    n_lat = x.shape[1]
    ROWS = n_lat // GRID_W
    rows = jnp.repeat(jnp.arange(ROWS, dtype=jnp.int32), GRID_W)
    cols = jnp.tile(jnp.arange(GRID_W, dtype=jnp.int32), ROWS)
    lb_all = jnp.cumsum(jax.nn.softmax(hgrn_lb_logits.astype(jnp.float32), axis=0), axis=0)
    lb_all = lb_all - lb_all[0]
    cond_lat = jax.nn.silu(c)
    cond_ctx = jax.nn.silu(c_ctx)
    lat, cst = x, ctx
    for l in range(DEPTH):
        last = l == DEPTH - 1
        mod_lat = jnp.split((cond_lat @ w_ada[l] + b_ada[l])[:, None, :], 6, axis=-1)
        mod_ctx = jnp.split(cond_ctx @ w_ada[l] + b_ada[l], 6, axis=-1)
        h_lat = modulate(rmsnorm(lat, g_pre_mix[l]), mod_lat[0], mod_lat[1])
        h_ctx = modulate(rmsnorm(cst, g_pre_mix[l]), mod_ctx[0], mod_ctx[1])
        m_lat, m_ctx = hybrid_mixer(
            h_lat, h_ctx, rows, cols, w_in[l], lru_conv_w[l], lru_conv_b[l], lru_rec_w[l],
            lru_rec_b[l], lru_inp_w[l], lru_inp_b[l], lru_lambda[l], attn_sink[l], lb_all[l],
            hgrn_norm_g[l], ssd_conv_w[l], ssd_conv_b[l], ssd_dt_bias[l], ssd_a_log[l],
            ssd_skip[l], ssd_norm_g[l], w_branch[l], w_out[l], not last)
        lat = lat + mod_lat[2] * rmsnorm(m_lat, g_post_mix[l])
        h2 = modulate(rmsnorm(lat, g_pre_mlp[l]), mod_lat[3], mod_lat[4])
        lat = lat + mod_lat[5] * rmsnorm(sqrelu_mlp(h2, w_mlp_up[l], w_mlp_down[l]), g_post_mlp[l])
        if not last:
            cst = cst + mod_ctx[2] * rmsnorm(m_ctx, g_post_mix[l])
            h2c = modulate(rmsnorm(cst, g_pre_mlp[l]), mod_ctx[3], mod_ctx[4])
            cst = cst + mod_ctx[5] * rmsnorm(sqrelu_mlp(h2c, w_mlp_up[l], w_mlp_down[l]),
                                              g_post_mlp[l])
    return lat
```

```python
import functools

import numpy as np
import jax
import jax.numpy as jnp
from jax import lax
from jax.experimental import pallas as pl
from jax.experimental.pallas import tpu as pltpu

f32 = jnp.float32
bf16 = jnp.bfloat16

D_MODEL = 1024
DEPTH = 2
GRID_W = 64
N_BRANCH = 4
BRANCH_WIDTH = D_MODEL // 2
CONV_WIDTH = 4
LRU_WIDTH = BRANCH_WIDTH
LRU_BLOCKS = 8
LRU_BLOCK_W = LRU_WIDTH // LRU_BLOCKS
LRU_C = 8.0
ATTN_HEAD_DIM = 64
ATTN_HEADS = BRANCH_WIDTH // ATTN_HEAD_DIM
ATTN_KV_HEADS = 2
ATTN_REP = ATTN_HEADS // ATTN_KV_HEADS
ATTN_WINDOW = 128
ATTN_BLOCK = 128
ROPE_BASE = 10000.0
HGRN_HEAD_DIM = 128
HGRN_HEADS = BRANCH_WIDTH // HGRN_HEAD_DIM
SSD_WIDTH = BRANCH_WIDTH
SSD_HEAD_DIM = 64
SSD_HEADS = SSD_WIDTH // SSD_HEAD_DIM
SSD_GROUPS = 2
SSD_HPG = SSD_HEADS // SSD_GROUPS
SSD_STATE = 128
MLP_HIDDEN = 4 * D_MODEL
NORM_EPS = 1e-6

IN_COLS = 10000
MERGE_COLS = N_BRANCH * D_MODEL
PROJ_COLS = 10240
LRU_BLK = 4
ATTN_Q_BLK = 10
ATTN_KV_BLK = 22
HGRN_BLK = (23, 25, 27, 29, 31)
SSD_Z_BLK = 33
SSD_X_BLK = 35
SSD_B_BLK = 74
SSD_C_BLK = 76
SSD_DT_BLK = 78

ROW_TILE = 256
SCAN_CHUNK = 256
CHUNK = 64
SEG_PAD = 8
HGRN_FAST_MIN_TOTAL = -160.0
VMEM_LIMIT = 58 * 1024 * 1024


def _cparams(n_axes):
    return pltpu.CompilerParams(dimension_semantics=("arbitrary",) * n_axes, vmem_limit_bytes=VMEM_LIMIT)


def _sigmoid(x):
    return jax.nn.sigmoid(x)


def _silu(x):
    return x * jax.nn.sigmoid(x)


def _gelu_tanh(x):
    return 0.5 * x * (1.0 + jnp.tanh(np.float32(np.sqrt(2.0 / np.pi)) * (x + 0.044715 * (x * x * x))))


def _expm1(x):
    poly = x * (1.0 + x * (1.0 / 2) * (1.0 + x * (1.0 / 3) * (1.0 + x * (1.0 / 4) * (1.0 + x * (1.0 / 5) * (
        1.0 + x * (1.0 / 6) * (1.0 + x * (1.0 / 7)))))))
    return jnp.where(jnp.abs(x) < 0.3, poly, jnp.exp(x) - 1.0)


def _log_sigmoid(x):
    return jnp.minimum(x, 0.0) - jnp.log1p(jnp.exp(-jnp.abs(x)))


def _softplus(x):
    return jnp.maximum(x, 0.0) + jnp.log1p(jnp.exp(-jnp.abs(x)))


def _rms(x, g):
    return x * lax.rsqrt(jnp.mean(x * x, axis=-1, keepdims=True) + NORM_EPS) * g


def _dot(a, b):
    return jnp.dot(a, b, preferred_element_type=f32)


def _dot_nt(a, b):
    return lax.dot_general(a, b, (((1,), (1,)), ((), ())), preferred_element_type=f32)


def _dot_tn(a, b):
    return lax.dot_general(a, b, (((0,), (0,)), ((), ())), preferred_element_type=f32)


def _split_dot(tri, x):
    hi = x.astype(bf16)
    lo = (x - hi.astype(f32)).astype(bf16)
    return _dot(tri, hi) + _dot(tri, lo)


def _tri(n, reverse):
    r = lax.broadcasted_iota(jnp.int32, (n, n), 0)
    c = lax.broadcasted_iota(jnp.int32, (n, n), 1)
    return (c >= r) if reverse else (c <= r)


def _ada_kernel(c_ref, w_ref, b_ref, o_ref):
    cond = _silu(c_ref[...])
    o_ref[...] = jnp.dot(cond, w_ref[...], preferred_element_type=f32,
                         precision=lax.Precision.HIGHEST) + b_ref[...]


def _ada(c_all, w_ada, b_ada):
    rows = c_all.shape[0]
    nt = 6 * D_MODEL // 1024
    return pl.pallas_call(
        _ada_kernel,
        grid=(DEPTH, nt),
        in_specs=[pl.BlockSpec((rows, D_MODEL), lambda l, j: (0, 0)),
                  pl.BlockSpec((None, D_MODEL, 1024), lambda l, j: (l, 0, j)),
                  pl.BlockSpec((None, 1, 1024), lambda l, j: (l, 0, j))],
        out_specs=pl.BlockSpec((None, rows, 1024), lambda l, j: (l, 0, j)),
        out_shape=jax.ShapeDtypeStruct((DEPTH, rows, 6 * D_MODEL), f32),
        compiler_params=_cparams(2), name="ada_mod",
    )(c_all, w_ada, b_ada.reshape(DEPTH, 1, 6 * D_MODEL))


def _mod_row_index(n_batch):
    return lambda b, i: (jnp.where(i == 0, n_batch, b), 0, 0)


def _norm_kernel(x_ref, m_ref, g_ref, o_ref):
    shift = m_ref[:, 0:D_MODEL]
    scale = m_ref[:, D_MODEL:2 * D_MODEL]
    h = _rms(x_ref[...], g_ref[...]) * (1.0 + scale) + shift
    o_ref[...] = h.astype(bf16)


def _norm_mod(xs, mods_l, g):
    B, T, _ = xs.shape
    return pl.pallas_call(
        _norm_kernel,
        grid=(B, T // ROW_TILE),
        in_specs=[pl.BlockSpec((None, ROW_TILE, D_MODEL), lambda b, i: (b, i, 0)),
                  pl.BlockSpec((None, 1, 6 * D_MODEL), _mod_row_index(B)),
                  pl.BlockSpec((1, D_MODEL), lambda b, i: (0, 0))],
        out_specs=pl.BlockSpec((None, ROW_TILE, D_MODEL), lambda b, i: (b, i, 0)),
        out_shape=jax.ShapeDtypeStruct((B, T, D_MODEL), bf16),
        compiler_params=_cparams(2), name="norm_mod",
    )(xs, mods_l, g)


def _proj_kernel(a_ref, w_ref, o_ref):
    o_ref[...] = _dot(a_ref[...], w_ref[...])


def _in_proj(h, w_perm):
    B, T, _ = h.shape
    tn = 1024
    return pl.pallas_call(
        _proj_kernel,
        grid=(B, PROJ_COLS // tn),
        in_specs=[pl.BlockSpec((None, T, D_MODEL), lambda b, j: (b, 0, 0)),
                  pl.BlockSpec((D_MODEL, tn), lambda b, j: (0, j))],
        out_specs=pl.BlockSpec((None, T, tn), lambda b, j: (b, 0, j)),
        out_shape=jax.ShapeDtypeStruct((B, T, PROJ_COLS), f32),
        compiler_params=_cparams(2), name="in_proj",
    )(h, w_perm)


def _stage_segments(dst, src_ref, col0, width, n_ctx, t_all):
    z = jnp.zeros((SEG_PAD, width), f32)
    dst[0:SEG_PAD, :] = z
    dst[SEG_PAD:SEG_PAD + n_ctx, :] = src_ref[0:n_ctx, col0:col0 + width]
    dst[SEG_PAD + n_ctx:2 * SEG_PAD + n_ctx, :] = z
    dst[2 * SEG_PAD + n_ctx:2 * SEG_PAD + t_all, :] = src_ref[n_ctx:t_all, col0:col0 + width]
    dst[2 * SEG_PAD + t_all:3 * SEG_PAD + t_all, :] = z


def _staged_window(j):
    return pl.multiple_of(SCAN_CHUNK * j + jnp.where(j >= 1, SEG_PAD, 0), SEG_PAD)


def _conv_chunk(stage_ref, win0, cw, cb):
    n = SCAN_CHUNK + 2 * SEG_PAD
    win = stage_ref[pl.ds(win0, n), :]
    acc = cb + cw[2:3, :] * win[SEG_PAD:SEG_PAD + SCAN_CHUNK]
    for tap, o in ((0, -2), (1, -1), (3, 1)):
        acc = acc + cw[tap:tap + 1, :] * pltpu.roll(win, (-o) % n, 0)[SEG_PAD:SEG_PAD + SCAN_CHUNK]
    return acc


def _lru_kernel(p_ref, cw_ref, cb_ref, wbd_ref, gb_ref, csp_ref, y_ref, stage, abuf, bbuf, hbuf, *, n_ctx):
    T = p_ref.shape[0]
    W = LRU_WIDTH
    n_chunks = T // SCAN_CHUNK
    n_groups = SCAN_CHUNK // 8
    _stage_segments(stage, p_ref, 0, W, n_ctx, T)
    cw = cw_ref[...]
    cb = cb_ref[...]
    r8 = lax.broadcasted_iota(jnp.int32, (SCAN_CHUNK, W), 0) & 7

    def chunk(j, carry, d):
        reverse = d == 1
        x = _conv_chunk(stage, _staged_window(j), cw, cb)
        g = _dot(x.astype(bf16), wbd_ref[:, 2 * W * d:2 * W * (d + 1)]) + gb_ref[:, 2 * W * d:2 * W * (d + 1)]
        r = _sigmoid(g[:, 0:W])
        i = _sigmoid(g[:, W:2 * W])
        log_a = -csp_ref[d:d + 1, :] * r
        a = jnp.exp(log_a)
        b = jnp.sqrt(-_expm1(2.0 * log_a)) * (i * x)
        for s in (1, 2, 4):
            if reverse:
                m = r8 < 8 - s
                sh = SCAN_CHUNK - s
            else:
                m = r8 >= s
                sh = s
            a_sh = jnp.where(m, pltpu.roll(a, sh, 0), 1.0)
            b_sh = jnp.where(m, pltpu.roll(b, sh, 0), 0.0)
            b = a * b_sh + b
            a = a * a_sh
        abuf[...] = a
        bbuf[...] = b

        def group(gi, c):
            gg = (n_groups - 1 - gi) if reverse else gi
            r0 = pl.multiple_of(gg * 8, 8)
            h = abuf[pl.ds(r0, 8), :] * c + bbuf[pl.ds(r0, 8), :]
            hbuf[pl.ds(r0, 8), :] = h
            edge = h[0:1, :] if reverse else h[7:8, :]
            return jnp.broadcast_to(edge, (8, W))

        return lax.fori_loop(0, n_groups, group, carry, unroll=4)

    zero = jnp.zeros((8, W), f32)

    def fwd_body(j, carry):
        carry = chunk(j, carry, 0)
        y_ref[pl.ds(pl.multiple_of(j * SCAN_CHUNK, SCAN_CHUNK), SCAN_CHUNK), :] = hbuf[...]
        return carry

    lax.fori_loop(0, n_chunks, fwd_body, zero)

    def bwd_body(k, carry):
        j = jnp.where(k == 0, 0, n_chunks - k)
        carry = chunk(j, carry, 1)
        rows = pl.ds(pl.multiple_of(j * SCAN_CHUNK, SCAN_CHUNK), SCAN_CHUNK)
        y_ref[rows, :] = (y_ref[rows, :] + hbuf[...]) * _gelu_tanh(p_ref[rows, W:2 * W])
        return carry

    lax.fori_loop(0, n_chunks, bwd_body, zero)


def _lru(proj, cw, cb, wbd, gb, csp, n_ctx):
    B, T, _ = proj.shape
    W = LRU_WIDTH
    full = lambda *s: pl.BlockSpec(s, lambda b: (0,) * len(s))
    return pl.pallas_call(
        functools.partial(_lru_kernel, n_ctx=n_ctx),
        grid=(B,),
        in_specs=[pl.BlockSpec((None, T, 2 * W), lambda b: (b, 0, LRU_BLK)),
                  full(CONV_WIDTH, W), full(1, W), full(W, 4 * W), full(1, 4 * W), full(2, W)],
        out_specs=pl.BlockSpec((None, T, W), lambda b: (b, 0, 0)),
        out_shape=jax.ShapeDtypeStruct((B, T, W), f32),
        scratch_shapes=[pltpu.VMEM((T + 3 * SEG_PAD, W), f32), pltpu.VMEM((SCAN_CHUNK, W), f32),
                        pltpu.VMEM((SCAN_CHUNK, W), f32), pltpu.VMEM((SCAN_CHUNK, W), f32)],
        compiler_params=_cparams(1), name="rglru",
    )(proj, cw, cb, wbd, gb, csp)


def _attn_kernel(sink_ref, q_ref, kv_ref, cos_ref, sin_ref, y_ref, qs, kp, vp, kc, vc, *, n_ctx):
    T = q_ref.shape[0]
    L = T - n_ctx
    hd = ATTN_HEAD_DIM
    G, R = ATTN_KV_HEADS, ATTN_REP
    nb = L // ATTN_BLOCK
    span = 3 * ATTN_BLOCK
    scale = np.float32(hd ** -0.5)

    def swap_halves(x):
        lane = lax.broadcasted_iota(jnp.int32, x.shape, 1)
        return jnp.where((lane & 31) < 16, pltpu.roll(x, 128 - 16, 1), pltpu.roll(x, 16, 1))

    zpad = jnp.zeros((ATTN_BLOCK, hd), bf16)
    for g in range(G):
        kp[g, 0:ATTN_BLOCK, :] = zpad
        kp[g, ATTN_BLOCK + L:2 * ATTN_BLOCK + L, :] = zpad
        vp[g, 0:ATTN_BLOCK, :] = zpad
        vp[g, ATTN_BLOCK + L:2 * ATTN_BLOCK + L, :] = zpad
        kc[g] = kv_ref[0:n_ctx, hd * g:hd * g + hd].astype(bf16)
        vc[g] = kv_ref[0:n_ctx, hd * (G + g):hd * (G + g) + hd].astype(bf16)

    def rope_rows(c, carry):
        r0 = pl.multiple_of(c * ATTN_BLOCK, ATTN_BLOCK)
        cos = cos_ref[pl.ds(r0, ATTN_BLOCK), :]
        sin = sin_ref[pl.ds(r0, ATTN_BLOCK), :]
        for s in range(ATTN_HEADS * hd // 128):
            xq = q_ref[pl.ds(n_ctx + r0, ATTN_BLOCK), 128 * s:128 * s + 128]
            xr = ((xq * cos + swap_halves(xq) * sin) * scale).astype(bf16)
            qs[2 * s, pl.ds(r0, ATTN_BLOCK), :] = xr[:, 0:hd]
            qs[2 * s + 1, pl.ds(r0, ATTN_BLOCK), :] = xr[:, hd:2 * hd]
        xk = kv_ref[pl.ds(n_ctx + r0, ATTN_BLOCK), 0:128]
        xr = (xk * cos + swap_halves(xk) * sin).astype(bf16)
        xv = kv_ref[pl.ds(n_ctx + r0, ATTN_BLOCK), 128:256].astype(bf16)
        for g in range(G):
            kp[g, pl.ds(ATTN_BLOCK + r0, ATTN_BLOCK), :] = xr[:, hd * g:hd * g + hd]
            vp[g, pl.ds(ATTN_BLOCK + r0, ATTN_BLOCK), :] = xv[:, hd * g:hd * g + hd]
        return carry

    lax.fori_loop(0, nb, rope_rows, 0)

    def sink_column(g, rows_per_head):
        rows = lax.broadcasted_iota(jnp.int32, (R * rows_per_head, 1), 0)
        col = jnp.full((R * rows_per_head, 1), sink_ref[R * g + R - 1], f32)
        for r in range(R - 2, -1, -1):
            col = jnp.where(rows < (r + 1) * rows_per_head, sink_ref[R * g + r], col)
        return col

    def softmax_pv(s, sink, v):
        m = jnp.maximum(jnp.max(s, axis=-1, keepdims=True), sink)
        p = jnp.exp(s - m)
        den = jnp.sum(p, axis=-1, keepdims=True) + jnp.exp(sink - m)
        return _dot((p / den).astype(bf16), v)

    qrow = lax.broadcasted_iota(jnp.int32, (R * ATTN_BLOCK, span), 0) & (ATTN_BLOCK - 1)
    kcol = lax.broadcasted_iota(jnp.int32, (R * ATTN_BLOCK, span), 1)
    band = (kcol - qrow >= ATTN_BLOCK - ATTN_WINDOW) & (kcol - qrow <= ATTN_BLOCK + ATTN_WINDOW)

    def block(n, carry):
        r0 = pl.multiple_of(n * ATTN_BLOCK, ATTN_BLOCK)
        kpos = kcol + (n - 1) * ATTN_BLOCK
        valid = band & (kpos >= 0) & (kpos < L)
        for g in range(G):
            q = jnp.concatenate([qs[R * g + r, pl.ds(r0, ATTN_BLOCK), :] for r in range(R)], axis=0)
            k = jnp.concatenate([kp[g, pl.ds(r0, span), :], kc[g]], axis=0)
            v = jnp.concatenate([vp[g, pl.ds(r0, span), :], vc[g]], axis=0)
            s = _dot_nt(q, k)
            s = jnp.concatenate([jnp.where(valid, s[:, 0:span], -jnp.inf), s[:, span:]], axis=1)
            o = softmax_pv(s, sink_column(g, ATTN_BLOCK), v)
            y_ref[pl.ds(n_ctx + r0, ATTN_BLOCK), R * hd * g:R * hd * (g + 1)] = jnp.concatenate(
                [o[ATTN_BLOCK * r:ATTN_BLOCK * (r + 1)] for r in range(R)], axis=1)
        return carry

    lax.fori_loop(0, nb, block, 0)

    for g in range(G):
        q = jnp.concatenate(
            [(q_ref[0:n_ctx, hd * (R * g + r):hd * (R * g + r + 1)] * scale).astype(bf16) for r in range(R)], axis=0)
        o = softmax_pv(_dot_nt(q, kc[g]), sink_column(g, n_ctx), vc[g])
        y_ref[0:n_ctx, R * hd * g:R * hd * (g + 1)] = jnp.concatenate(
            [o[n_ctx * r:n_ctx * (r + 1)] for r in range(R)], axis=1)


def _attn(proj, sink, cos_t, sin_t, n_ctx):
    B, T, _ = proj.shape
    L = T - n_ctx
    W = BRANCH_WIDTH
    hd = ATTN_HEAD_DIM
    return pl.pallas_call(
        functools.partial(_attn_kernel, n_ctx=n_ctx),
        grid=(B,),
        in_specs=[pl.BlockSpec(memory_space=pltpu.SMEM),
                  pl.BlockSpec((None, T, W), lambda b: (b, 0, ATTN_Q_BLK)),
                  pl.BlockSpec((None, T, 256), lambda b: (b, 0, ATTN_KV_BLK)),
                  pl.BlockSpec((L, 128), lambda b: (0, 0)),
                  pl.BlockSpec((L, 128), lambda b: (0, 0))],
        out_specs=pl.BlockSpec((None, T, W), lambda b: (b, 0, 0)),
        out_shape=jax.ShapeDtypeStruct((B, T, W), f32),
        scratch_shapes=[pltpu.VMEM((ATTN_HEADS, L, hd), bf16),
                        pltpu.VMEM((ATTN_KV_HEADS, L + 2 * ATTN_BLOCK, hd), bf16),
                        pltpu.VMEM((ATTN_KV_HEADS, L + 2 * ATTN_BLOCK, hd), bf16),
                        pltpu.VMEM((ATTN_KV_HEADS, n_ctx, hd), bf16),
                        pltpu.VMEM((ATTN_KV_HEADS, n_ctx, hd), bf16)],
        compiler_params=_cparams(1), name="window_attn",
    )(sink, proj, proj, cos_t, sin_t)


def _hgrn_kernel(q_ref, i_ref, ff_ref, fb_ref, g_ref, llb_ref, lrest_ref, gain_ref, y_ref,
                 qv, lf0, lf1, yf, yb, st, *, n_ctx):
    T = q_ref.shape[0]
    K = HGRN_HEAD_DIM
    HP = 2
    n_chunks = T // CHUNK
    n_ctx_chunks = n_ctx // CHUNK
    lf_refs = (lf0, lf1)
    y_refs = (yf, yb)

    llb = llb_ref[...]
    lrest = lrest_ref[...]

    def prologue(c, carry):
        rows = pl.ds(pl.multiple_of(c * ROW_TILE, ROW_TILE), ROW_TILE)
        qv[rows, :] = _silu(q_ref[rows, :])
        for d, z_ref in enumerate((ff_ref, fb_ref)):
            lf_refs[d][rows, :] = jnp.logaddexp(llb, lrest + _log_sigmoid(z_ref[rows, :]))
        return carry

    lax.fori_loop(0, T // ROW_TILE, prologue, 0)
    st[...] = jnp.zeros(st.shape, f32)

    def chunk_of(k, d):
        if d == 0:
            return k
        return jnp.where(k < n_ctx_chunks, n_ctx_chunks - 1 - k, n_chunks + n_ctx_chunks - 1 - k)

    tril = {d: _tri(CHUNK, d == 1) for d in (0, 1)}
    tril_bf = {d: tril[d].astype(bf16) for d in (0, 1)}
    rows_k = lax.broadcasted_iota(jnp.int32, (CHUNK, K), 0)
    cols_c = lax.broadcasted_iota(jnp.int32, (CHUNK, CHUNK), 1)

    def exact_scores(q, kk, cum, d):
        def col(s, acc):
            sel = rows_k == s
            cs = jnp.sum(jnp.where(sel, cum, 0.0), axis=0, keepdims=True)
            ks = jnp.sum(jnp.where(sel, kk, 0.0), axis=0, keepdims=True)
            w = jnp.exp(jnp.minimum(cum - cs, 0.0))
            return jnp.where(cols_c == s, jnp.sum(q * ks * w, axis=1, keepdims=True), acc)
        return lax.fori_loop(0, CHUNK, col, jnp.zeros((CHUNK, CHUNK), f32))

    def step(k, d, h, exact):
        c = chunk_of(k, d)
        rows = pl.ds(pl.multiple_of(c * CHUNK, CHUNK), CHUNK)
        lanes = slice(K * h, K * (h + 1))
        lf = lf_refs[d][rows, lanes]
        kk = -_expm1(lf)
        q = qv[rows, lanes]
        v = i_ref[rows, lanes].astype(bf16)
        cum = _split_dot(tril_bf[d], lf)
        tot = cum[0:1, :] if d == 1 else cum[CHUNK - 1:CHUNK, :]
        if exact:
            a = exact_scores(q, kk, cum, d)
            e_in = jnp.exp(cum)
            e_out = jnp.exp(tot - cum)
        else:
            r = 0.5 * tot
            e1 = jnp.exp(cum - r)
            e2 = jnp.exp(r - cum)
            er = jnp.exp(r)
            a = _dot_nt((q * e1).astype(bf16), (kk * e2).astype(bf16))
            e_in = e1 * er
            e_out = e2 * er
        a = jnp.where(tril[d], a, 0.0)
        s_old = st[2 * h + d]
        o = _dot(a.astype(bf16), v) + _dot_nt((q * e_in).astype(bf16), s_old.astype(bf16))
        y_refs[d][rows, lanes] = o
        st[2 * h + d] = s_old * jnp.exp(tot) + _dot_tn(v, (kk * e_out).astype(bf16))

    def min_total(d):
        lf = lf_refs[d][...].reshape(n_chunks, CHUNK, HP * K)
        return jnp.min(jnp.sum(lf, axis=1))

    for d in (0, 1):
        fast_ok = min_total(d) > HGRN_FAST_MIN_TOTAL

        def run(exact, d=d):
            def body(k, carry):
                for h in range(HP):
                    step(k, d, h, exact)
                return carry
            lax.fori_loop(0, n_chunks, body, 0)

        pl.when(fast_ok)(functools.partial(run, False))
        pl.when(jnp.logical_not(fast_ok))(functools.partial(run, True))

    gain = gain_ref[...]

    def readout(c, carry):
        rows = pl.ds(pl.multiple_of(c * ROW_TILE, ROW_TILE), ROW_TILE)
        o = yf[rows, :] + yb[rows, :]
        gate = _silu(g_ref[rows, :])
        y_ref[rows, :] = jnp.concatenate(
            [_rms(o[:, K * h:K * (h + 1)], gain[:, K * h:K * (h + 1)]) for h in range(HP)], axis=1) * gate
        return carry

    lax.fori_loop(0, T // ROW_TILE, readout, 0)


def _hgrn(proj, log_lb, log_rest, gain, n_ctx):
    B, T, _ = proj.shape
    wblk = 2 * HGRN_HEAD_DIM
    col = lambda base: pl.BlockSpec((None, T, wblk), lambda b, p: (b, 0, base + p))
    par = pl.BlockSpec((1, wblk), lambda b, p: (0, p))
    return pl.pallas_call(
        functools.partial(_hgrn_kernel, n_ctx=n_ctx),
        grid=(B, HGRN_HEADS // 2),
        in_specs=[col(b0) for b0 in HGRN_BLK] + [par, par, par],
        out_specs=pl.BlockSpec((None, T, wblk), lambda b, p: (b, 0, p)),
        out_shape=jax.ShapeDtypeStruct((B, T, BRANCH_WIDTH), f32),
        scratch_shapes=[pltpu.VMEM((T, wblk), f32)] * 5 + [pltpu.VMEM((4, HGRN_HEAD_DIM, HGRN_HEAD_DIM), f32)],
        compiler_params=_cparams(2), name="hgrn2",
    )(proj, proj, proj, proj, proj, log_lb, log_rest, gain)


def _ssd_kernel(z_ref, x_ref, b_ref, c_ref, dt_ref, cwx_ref, cbx_ref, cwb_ref, cbb_ref, cwc_ref, cbc_ref,
                dtbias_ref, arow_ref, skip_ref, gain_ref, y_ref,
                xstage, bstage, cstage, xs, bs, cs, dtv, dta, yf, yb, st, *, n_ctx):
    T = x_ref.shape[0]
    P = SSD_HEAD_DIM
    R = SSD_HPG
    XW = R * P
    n_chunks = T // CHUNK
    n_ctx_chunks = n_ctx // CHUNK
    y_refs = (yf, yb)

    _stage_segments(xstage, x_ref, 0, XW, n_ctx, T)
    _stage_segments(bstage, b_ref, 0, SSD_STATE, n_ctx, T)
    _stage_segments(cstage, c_ref, 0, SSD_STATE, n_ctx, T)
    cwx, cbx = cwx_ref[...], cbx_ref[...]
    cwb, cbb = cwb_ref[...], cbb_ref[...]
    cwc, cbc = cwc_ref[...], cbc_ref[...]
    dtbias, arow = dtbias_ref[...], arow_ref[...]

    def prologue(j, carry):
        off = _staged_window(j)
        rows = pl.ds(pl.multiple_of(j * SCAN_CHUNK, SCAN_CHUNK), SCAN_CHUNK)
        xs[rows, :] = _silu(_conv_chunk(xstage, off, cwx, cbx))
        bs[rows, :] = _silu(_conv_chunk(bstage, off, cwb, cbb)).astype(bf16)
        cs[rows, :] = _silu(_conv_chunk(cstage, off, cwc, cbc)).astype(bf16)
        dt = _softplus(dt_ref[rows, :] + dtbias)
        dtv[rows, :] = dt
        dta[rows, :] = dt * arow
        return carry

    lax.fori_loop(0, T // SCAN_CHUNK, prologue, 0)
    st[...] = jnp.zeros(st.shape, f32)

    def chunk_of(k, d):
        if d == 0:
            return k
        return jnp.where(k < n_ctx_chunks, n_ctx_chunks - 1 - k, n_chunks + n_ctx_chunks - 1 - k)

    rr = lax.broadcasted_iota(jnp.int32, (CHUNK, CHUNK), 0)
    cc = lax.broadcasted_iota(jnp.int32, (CHUNK, CHUNK), 1)
    incl = {0: cc <= rr, 1: cc >= rr}
    incl_bf = {d: incl[d].astype(bf16) for d in (0, 1)}
    strict = {0: rr > cc, 1: rr < cc}

    def step(k, d):
        c = chunk_of(k, d)
        rows = pl.ds(pl.multiple_of(c * CHUNK, CHUNK), CHUNK)
        cm = cs[rows, :]
        bm = bs[rows, :]
        x = xs[rows, :]
        dt = dtv[rows, :]
        da = dta[rows, :]
        cum = _split_dot(incl_bf[d], da)
        tot = cum[0:1, :] if d == 1 else cum[CHUNK - 1:CHUNK, :]
        e_in = jnp.exp(cum)
        e_out = jnp.exp(tot - cum)
        e_tot = jnp.exp(tot)
        cb = _dot_nt(cm, bm)
        m = jnp.concatenate(
            [jnp.where(strict[d], jnp.broadcast_to(da[:, R * d + r:R * d + r + 1], (CHUNK, CHUNK)), 0.0)
             for r in range(R)], axis=1)
        logdec = _split_dot(incl_bf[d], m)
        s_old = st[d]
        inter = _dot(cm, s_old.astype(bf16))
        outs, wxs, tots = [], [], []
        for r in range(R):
            ln = R * d + r
            col = lambda a: a[:, ln:ln + 1]
            xr = x[:, P * r:P * (r + 1)] * col(dt)
            dec = jnp.where(incl[d], jnp.exp(logdec[:, CHUNK * r:CHUNK * (r + 1)]), 0.0)
            intra = _dot((cb * dec).astype(bf16), xr.astype(bf16))
            outs.append(intra + inter[:, P * r:P * (r + 1)] * col(e_in))
            wxs.append(xr * col(e_out))
            tots.append(jnp.broadcast_to(e_tot[:, ln:ln + 1], (SSD_STATE, P)))
        y_refs[d][rows, :] = jnp.concatenate(outs, axis=1)
        st[d] = s_old * jnp.concatenate(tots, axis=1) + _dot_tn(bm, jnp.concatenate(wxs, axis=1).astype(bf16))

    def body(k, carry):
        step(k, 0)
        step(k, 1)
        return carry

    lax.fori_loop(0, n_chunks, body, 0)

    skip, gain = skip_ref[...], gain_ref[...]

    def readout(c, carry):
        rows = pl.ds(pl.multiple_of(c * ROW_TILE, ROW_TILE), ROW_TILE)
        y = skip * xs[rows, :] + yf[rows, :] + yb[rows, :]
        y_ref[rows, :] = _rms(y * _silu(z_ref[rows, :]), gain)
        return carry

    lax.fori_loop(0, T // ROW_TILE, readout, 0)


def _ssd(proj, cwx, cbx, cwb, cbb, cwc, cbc, dtbias, arow, skip, gain, n_ctx):
    B, T, _ = proj.shape
    XW = SSD_HPG * SSD_HEAD_DIM
    N = SSD_STATE
    col = lambda base, w: pl.BlockSpec((None, T, w), lambda b, g: (b, 0, base + g))
    par = lambda r, w: pl.BlockSpec((None, r, w), lambda b, g: (g, 0, 0))
    return pl.pallas_call(
        functools.partial(_ssd_kernel, n_ctx=n_ctx),
        grid=(B, SSD_GROUPS),
        in_specs=[col(SSD_Z_BLK, XW), col(SSD_X_BLK, XW), col(SSD_B_BLK, N), col(SSD_C_BLK, N), col(SSD_DT_BLK, 128),
                  par(CONV_WIDTH, XW), par(1, XW), par(CONV_WIDTH, N), par(1, N), par(CONV_WIDTH, N), par(1, N),
                  par(1, 128), par(1, 128), par(1, XW), par(1, XW)],
        out_specs=pl.BlockSpec((None, T, XW), lambda b, g: (b, 0, g)),
        out_shape=jax.ShapeDtypeStruct((B, T, SSD_WIDTH), f32),
        scratch_shapes=[pltpu.VMEM((T + 3 * SEG_PAD, XW), f32), pltpu.VMEM((T + 3 * SEG_PAD, N), f32),
                        pltpu.VMEM((T + 3 * SEG_PAD, N), f32),
                        pltpu.VMEM((T, XW), f32), pltpu.VMEM((T, N), bf16), pltpu.VMEM((T, N), bf16),
                        pltpu.VMEM((T, 128), f32), pltpu.VMEM((T, 128), f32),
                        pltpu.VMEM((T, XW), f32), pltpu.VMEM((T, XW), f32),
                        pltpu.VMEM((2, N, XW), f32)],
        compiler_params=_cparams(2), name="ssd",
    )(proj, proj, proj, proj, proj, cwx, cbx, cwb, cbb, cwc, cbc, dtbias, arow, skip, gain)


def _post_kernel(x_ref, gl_ref, ya_ref, yb_ref, yc_ref, yd_ref, m_ref, wbr_ref, wout_ref, wup_ref, wdn_ref,
                 g1_ref, g2_ref, g3_ref, o_ref):
    D = D_MODEL
    ys = (ya_ref, yb_ref, yc_ref, yd_ref)
    merged = None
    for i in range(N_BRANCH):
        t = _sigmoid(gl_ref[:, D * i:D * (i + 1)]) * _dot(ys[i][...].astype(bf16), wbr_ref[i])
        merged = t if merged is None else merged + t
    mix = _dot(merged.astype(bf16), wout_ref[...])
    mod = lambda k: m_ref[:, D * k:D * (k + 1)]
    x1 = x_ref[...] + mod(2) * _rms(mix, g1_ref[...])
    h2 = (_rms(x1, g2_ref[...]) * (1.0 + mod(4)) + mod(3)).astype(bf16)
    down = None
    hc = 1024
    for c in range(MLP_HIDDEN // hc):
        u = jnp.maximum(_dot(h2, wup_ref[:, hc * c:hc * (c + 1)]), 0.0)
        t = _dot((u * u).astype(bf16), wdn_ref[hc * c:hc * (c + 1), :])
        down = t if down is None else down + t
    o_ref[...] = x1 + mod(5) * _rms(down, g3_ref[...])


def _post(xs, proj, ya, yb, yc, yd, mods_l, wbr, wout, wup, wdn, g1, g2, g3, skip_ctx):
    B, T, D = xs.shape
    W = BRANCH_WIDTH
    nt = T // ROW_TILE
    first = 1 if skip_ctx else 0
    tile = lambda w: pl.BlockSpec((None, ROW_TILE, w), lambda b, i: (b, i + first, 0))
    once = lambda *s: pl.BlockSpec(s, lambda b, i: (0,) * len(s), pipeline_mode=pl.Buffered(1))
    mod_idx = _mod_row_index(B)
    return pl.pallas_call(
        _post_kernel,
        grid=(B, nt - first),
        in_specs=[tile(D), tile(MERGE_COLS), tile(W), tile(W), tile(W), tile(W),
                  pl.BlockSpec((None, 1, 6 * D), lambda b, i: mod_idx(b, i + first)),
                  once(N_BRANCH, W, D), once(D, D), once(D, MLP_HIDDEN), once(MLP_HIDDEN, D),
                  once(1, D), once(1, D), once(1, D)],
        out_specs=pl.BlockSpec((None, ROW_TILE, D), lambda b, i: (b, i, 0)),
        out_shape=jax.ShapeDtypeStruct((B, T - first * ROW_TILE, D), f32),
        compiler_params=_cparams(2), name="merge_mlp",
    )(xs, proj, ya, yb, yc, yd, mods_l, wbr, wout, wup, wdn, g1, g2, g3)


def _block_diag(w):
    eye = jnp.eye(LRU_BLOCKS, dtype=w.dtype)
    return jnp.einsum('hij,hg->higj', w, eye).reshape(LRU_WIDTH, LRU_WIDTH)


def _rope_tables(n_lat):
    half = ATTN_HEAD_DIM // 2
    quarter = half // 2
    inv_freq = ROPE_BASE ** (-jnp.arange(quarter, dtype=f32) / quarter)
    t = jnp.arange(n_lat, dtype=jnp.int32)
    rows = (t // GRID_W).astype(f32)[:, None] * inv_freq
    cols = (t % GRID_W).astype(f32)[:, None] * inv_freq
    cos_h = jnp.concatenate([jnp.cos(rows), jnp.cos(rows), jnp.cos(cols), jnp.cos(cols)], axis=-1)
    sin_h = jnp.concatenate([-jnp.sin(rows), jnp.sin(rows), -jnp.sin(cols), jnp.sin(cols)], axis=-1)
    return jnp.tile(cos_h, (1, 2)), jnp.tile(sin_h, (1, 2))


def kernel(x, c, ctx, c_ctx, w_ada, b_ada, g_pre_mix, g_post_mix, g_pre_mlp, g_post_mlp, w_in, lru_conv_w, lru_conv_b, lru_rec_w, lru_rec_b, lru_inp_w, lru_inp_b, lru_lambda, attn_sink, hgrn_lb_logits, hgrn_norm_g, ssd_conv_w, ssd_conv_b, ssd_dt_bias, ssd_a_log, ssd_skip, ssd_norm_g, w_branch, w_out, w_mlp_up, w_mlp_down):
    B, L, D = x.shape
    n_ctx = ctx.shape[1]
    assert D == D_MODEL and n_ctx == ROW_TILE == SCAN_CHUNK and L % ROW_TILE == 0 and L % GRID_W == 0
    assert w_in.shape == (DEPTH, D_MODEL, IN_COLS)

    xs = jnp.concatenate([ctx, x], axis=1)
    pad_rows = (-(B + 1)) % 8
    c_all = jnp.concatenate([c, c_ctx[None, :], jnp.zeros((pad_rows, D), f32)], axis=0)
    mods = _ada(c_all, w_ada, b_ada)
    mods = mods.reshape(DEPTH, c_all.shape[0], 1, 6 * D)

    lb = jnp.cumsum(jax.nn.softmax(hgrn_lb_logits.astype(f32), axis=0), axis=0)
    lb = lb - lb[0]
    cos_t, sin_t = _rope_tables(L)
    G, R, P, N = SSD_GROUPS, SSD_HPG, SSD_HEAD_DIM, SSD_STATE

    for l in range(DEPTH):
        last = l == DEPTH - 1
        wl = w_in[l]
        dtf, dtb = wl[:, 5888:5896], wl[:, 5896:5904]
        zpad = jnp.zeros((D, 128 - 2 * R), f32)
        dt_cols = [jnp.concatenate([dtf[:, R * g:R * (g + 1)], dtb[:, R * g:R * (g + 1)], zpad], axis=1)
                   for g in range(G)]
        w_perm = jnp.concatenate([wl[:, 5904:], wl[:, :5888]] + dt_cols, axis=1).astype(bf16)

        h = _norm_mod(xs, mods[l], g_pre_mix[l][None, :])
        proj = _in_proj(h, w_perm)

        wbd = jnp.concatenate([_block_diag(lru_rec_w[l, 0]), _block_diag(lru_inp_w[l, 0]),
                               _block_diag(lru_rec_w[l, 1]), _block_diag(lru_inp_w[l, 1])], axis=1).astype(bf16)
        gb = jnp.concatenate([lru_rec_b[l, 0], lru_inp_b[l, 0], lru_rec_b[l, 1], lru_inp_b[l, 1]])[None, :]
        csp = LRU_C * jax.nn.softplus(-lru_lambda[l])
        ya = _lru(proj, lru_conv_w[l], lru_conv_b[l][None, :], wbd, gb, csp, n_ctx)

        yb = _attn(proj, attn_sink[l], cos_t, sin_t, n_ctx)

        lbl = lb[l][None, :]
        yc = _hgrn(proj, jnp.log(lbl), jnp.log1p(-lbl), hgrn_norm_g[l][None, :], n_ctx)

        cw, cbv = ssd_conv_w[l], ssd_conv_b[l]
        XW = R * P
        grp = lambda a, off, w: jnp.stack([a[..., off + w * g:off + w * (g + 1)] for g in range(G)], axis=0)
        cwx, cbx = grp(cw, 0, XW), grp(cbv[None, :], 0, XW)
        cwb, cbb = grp(cw, SSD_WIDTH, N), grp(cbv[None, :], SSD_WIDTH, N)
        cwc, cbc = grp(cw, SSD_WIDTH + G * N, N), grp(cbv[None, :], SSD_WIDTH + G * N, N)
        a_neg = -jnp.exp(ssd_a_log[l].astype(f32))
        lanes = lambda v: jnp.stack(
            [jnp.concatenate([v[0, R * g:R * (g + 1)], v[1, R * g:R * (g + 1)], jnp.zeros((128 - 2 * R,), f32)])[None, :]
             for g in range(G)], axis=0)
        skip = jnp.repeat(ssd_skip[l], P).reshape(G, 1, XW)
        gain_d = ssd_norm_g[l].reshape(G, 1, XW)
        yd = _ssd(proj, cwx, cbx, cwb, cbb, cwc, cbc, lanes(ssd_dt_bias[l]), lanes(a_neg), skip, gain_d, n_ctx)

        xs = _post(xs, proj, ya, yb, yc, yd, mods[l], w_branch[l].astype(bf16), w_out[l].astype(bf16),
                   w_mlp_up[l].astype(bf16), w_mlp_down[l].astype(bf16),
                   g_post_mix[l][None, :], g_pre_mlp[l][None, :], g_post_mlp[l][None, :], skip_ctx=last)
    return xs
```

```python
import functools

import numpy as np
import jax
import jax.numpy as jnp
from jax import lax
from jax.experimental import pallas as pl
from jax.experimental.pallas import tpu as pltpu

f32 = jnp.float32
bf16 = jnp.bfloat16

D_MODEL = 1024
DEPTH = 2
GRID_W = 64
N_BRANCH = 4
BRANCH_WIDTH = D_MODEL // 2
CONV_WIDTH = 4
LRU_WIDTH = BRANCH_WIDTH
LRU_BLOCKS = 8
LRU_BLOCK_W = LRU_WIDTH // LRU_BLOCKS
LRU_C = 8.0
ATTN_HEAD_DIM = 64
ATTN_HEADS = BRANCH_WIDTH // ATTN_HEAD_DIM
ATTN_KV_HEADS = 2
ATTN_REP = ATTN_HEADS // ATTN_KV_HEADS
ATTN_WINDOW = 128
ATTN_BLOCK = 128
ROPE_BASE = 10000.0
HGRN_HEAD_DIM = 128
HGRN_HEADS = BRANCH_WIDTH // HGRN_HEAD_DIM
SSD_WIDTH = BRANCH_WIDTH
SSD_HEAD_DIM = 64
SSD_HEADS = SSD_WIDTH // SSD_HEAD_DIM
SSD_GROUPS = 2
SSD_HPG = SSD_HEADS // SSD_GROUPS
SSD_STATE = 128
MLP_HIDDEN = 4 * D_MODEL
NORM_EPS = 1e-6

IN_COLS = 10000
MERGE_COLS = N_BRANCH * D_MODEL
PROJ_COLS = 10240
LRU_BLK = 4
ATTN_Q_BLK = 10
ATTN_KV_BLK = 22
HGRN_BLK = (23, 25, 27, 29, 31)
SSD_Z_BLK = 33
SSD_X_BLK = 35
SSD_B_BLK = 74
SSD_C_BLK = 76
SSD_DT_BLK = 78

ROW_TILE = 256
SCAN_CHUNK = 256
CHUNK = 64
SEG_PAD = 8
HGRN_FAST_MIN_TOTAL = -160.0
HGRN_UNROLL = 2
SSD_UNROLL = 2
MASK_BIAS = -1e30
VMEM_LIMIT = 58 * 1024 * 1024


def _cparams(n_axes):
    return pltpu.CompilerParams(dimension_semantics=("arbitrary",) * n_axes, vmem_limit_bytes=VMEM_LIMIT)


def _sigmoid(x):
    return 0.5 * jnp.tanh(0.5 * x) + 0.5


def _silu(x):
    return x * jax.nn.sigmoid(x)


def _gelu_tanh(x):
    return 0.5 * x * (1.0 + jnp.tanh(np.float32(np.sqrt(2.0 / np.pi)) * (x + 0.044715 * (x * x * x))))


def _expm1_given_exp(x, ex):
    poly = x * (1.0 + x * (1.0 / 2) * (1.0 + x * (1.0 / 3) * (1.0 + x * (1.0 / 4))))
    return jnp.where(jnp.abs(x) < 0.03, poly, ex - 1.0)


def _log_sigmoid(x):
    return jnp.minimum(x, 0.0) - jnp.log(1.0 + jnp.exp(-jnp.abs(x)))


def _logaddexp(a, b):
    return jnp.maximum(a, b) + jnp.log(1.0 + jnp.exp(-jnp.abs(a - b)))


def _softplus(x):
    return jnp.maximum(x, 0.0) + jnp.log1p(jnp.exp(-jnp.abs(x)))


def _rms(x, g):
    return x * lax.rsqrt(jnp.mean(x * x, axis=-1, keepdims=True) + NORM_EPS) * g


def _dot(a, b):
    return jnp.dot(a, b, preferred_element_type=f32)


def _dot_nt(a, b):
    return lax.dot_general(a, b, (((1,), (1,)), ((), ())), preferred_element_type=f32)


def _dot_tn(a, b):
    return lax.dot_general(a, b, (((0,), (0,)), ((), ())), preferred_element_type=f32)


def _split_dot(tri, x):
    hi = x.astype(bf16)
    lo = (x - hi.astype(f32)).astype(bf16)
    return _dot(tri, hi) + _dot(tri, lo)


def _split_dot_right(x, sel):
    hi = x.astype(bf16)
    lo = (x - hi.astype(f32)).astype(bf16)
    return _dot(hi, sel) + _dot(lo, sel)


def _tri(n, reverse):
    r = lax.broadcasted_iota(jnp.int32, (n, n), 0)
    c = lax.broadcasted_iota(jnp.int32, (n, n), 1)
    return (c >= r) if reverse else (c <= r)


def _ada_kernel(c_ref, w_ref, b_ref, o_ref):
    cond = _silu(c_ref[...])
    o_ref[...] = jnp.dot(cond, w_ref[...], preferred_element_type=f32,
                         precision=lax.Precision.HIGHEST) + b_ref[...]


def _ada(c_all, w_ada, b_ada):
    rows = c_all.shape[0]
    nt = 6 * D_MODEL // 1024
    return pl.pallas_call(
        _ada_kernel,
        grid=(DEPTH, nt),
        in_specs=[pl.BlockSpec((rows, D_MODEL), lambda l, j: (0, 0)),
                  pl.BlockSpec((None, D_MODEL, 1024), lambda l, j: (l, 0, j)),
                  pl.BlockSpec((None, 1, 1024), lambda l, j: (l, 0, j))],
        out_specs=pl.BlockSpec((None, rows, 1024), lambda l, j: (l, 0, j)),
        out_shape=jax.ShapeDtypeStruct((DEPTH, rows, 6 * D_MODEL), f32),
        compiler_params=_cparams(2), name="ada_mod",
    )(c_all, w_ada, b_ada.reshape(DEPTH, 1, 6 * D_MODEL))


def _mod_row_index(n_batch):
    return lambda b, i: (jnp.where(i == 0, n_batch, b), 0, 0)


def _norm_kernel(x_ref, m_ref, g_ref, o_ref):
    shift = m_ref[:, 0:D_MODEL]
    scale = m_ref[:, D_MODEL:2 * D_MODEL]
    h = _rms(x_ref[...], g_ref[...]) * (1.0 + scale) + shift
    o_ref[...] = h.astype(bf16)


def _norm_mod(xs, mods_l, g):
    B, T, _ = xs.shape
    return pl.pallas_call(
        _norm_kernel,
        grid=(B, T // ROW_TILE),
        in_specs=[pl.BlockSpec((None, ROW_TILE, D_MODEL), lambda b, i: (b, i, 0)),
                  pl.BlockSpec((None, 1, 6 * D_MODEL), _mod_row_index(B)),
                  pl.BlockSpec((1, D_MODEL), lambda b, i: (0, 0))],
        out_specs=pl.BlockSpec((None, ROW_TILE, D_MODEL), lambda b, i: (b, i, 0)),
        out_shape=jax.ShapeDtypeStruct((B, T, D_MODEL), bf16),
        compiler_params=_cparams(2), name="norm_mod",
    )(xs, mods_l, g)


def _proj_kernel(a_ref, w_ref, o_ref):
    o_ref[...] = _dot(a_ref[...], w_ref[...])


def _in_proj(h, w_perm):
    B, T, _ = h.shape
    tn = 1024
    return pl.pallas_call(
        _proj_kernel,
        grid=(B, PROJ_COLS // tn),
        in_specs=[pl.BlockSpec((None, T, D_MODEL), lambda b, j: (b, 0, 0)),
                  pl.BlockSpec((D_MODEL, tn), lambda b, j: (0, j))],
        out_specs=pl.BlockSpec((None, T, tn), lambda b, j: (b, 0, j)),
        out_shape=jax.ShapeDtypeStruct((B, T, PROJ_COLS), f32),
        compiler_params=_cparams(2), name="in_proj",
    )(h, w_perm)


def _stage_segments(dst, src_ref, col0, width, n_ctx, t_all):
    z = jnp.zeros((SEG_PAD, width), f32)
    dst[0:SEG_PAD, :] = z
    dst[SEG_PAD:SEG_PAD + n_ctx, :] = src_ref[0:n_ctx, col0:col0 + width]
    dst[SEG_PAD + n_ctx:2 * SEG_PAD + n_ctx, :] = z
    dst[2 * SEG_PAD + n_ctx:2 * SEG_PAD + t_all, :] = src_ref[n_ctx:t_all, col0:col0 + width]
    dst[2 * SEG_PAD + t_all:3 * SEG_PAD + t_all, :] = z


def _staged_window(j):
    return pl.multiple_of(SCAN_CHUNK * j + jnp.where(j >= 1, SEG_PAD, 0), SEG_PAD)


def _conv_chunk(stage_ref, win0, cw, cb):
    n = SCAN_CHUNK + 2 * SEG_PAD
    win = stage_ref[pl.ds(win0, n), :]
    acc = cb + cw[2:3, :] * win[SEG_PAD:SEG_PAD + SCAN_CHUNK]
    for tap, o in ((0, -2), (1, -1), (3, 1)):
        acc = acc + cw[tap:tap + 1, :] * pltpu.roll(win, (-o) % n, 0)[SEG_PAD:SEG_PAD + SCAN_CHUNK]
    return acc


def _lru_kernel(p_ref, cw_ref, cb_ref, wbd_ref, gb_ref, csp_ref, y_ref, stage, abuf, bbuf, hbuf, *, n_ctx):
    T = p_ref.shape[0]
    W = LRU_WIDTH
    n_chunks = T // SCAN_CHUNK
    n_groups = SCAN_CHUNK // 8
    _stage_segments(stage, p_ref, 0, W, n_ctx, T)
    cw = cw_ref[...]
    cb = cb_ref[...]
    r8 = lax.broadcasted_iota(jnp.int32, (SCAN_CHUNK, W), 0) & 7

    def chunk(j, carry, d):
        reverse = d == 1
        x = _conv_chunk(stage, _staged_window(j), cw, cb)
        g = _dot(x.astype(bf16), wbd_ref[:, 2 * W * d:2 * W * (d + 1)]) + gb_ref[:, 2 * W * d:2 * W * (d + 1)]
        r = _sigmoid(g[:, 0:W])
        i = _sigmoid(g[:, W:2 * W])
        log_a = -csp_ref[d:d + 1, :] * r
        a = jnp.exp(log_a)
        b = jnp.sqrt(-_expm1_given_exp(2.0 * log_a, a * a)) * (i * x)
        for s in (1, 2, 4):
            if reverse:
                m = r8 < 8 - s
                sh = SCAN_CHUNK - s
            else:
                m = r8 >= s
                sh = s
            a_sh = jnp.where(m, pltpu.roll(a, sh, 0), 1.0)
            b_sh = jnp.where(m, pltpu.roll(b, sh, 0), 0.0)
            b = a * b_sh + b
            a = a * a_sh
        abuf[...] = a
        bbuf[...] = b

        def group(gi, c):
            gg = (n_groups - 1 - gi) if reverse else gi
            r0 = pl.multiple_of(gg * 8, 8)
            h = abuf[pl.ds(r0, 8), :] * c + bbuf[pl.ds(r0, 8), :]
            hbuf[pl.ds(r0, 8), :] = h
            edge = h[0:1, :] if reverse else h[7:8, :]
            return jnp.broadcast_to(edge, (8, W))

        return lax.fori_loop(0, n_groups, group, carry, unroll=4)

    zero = jnp.zeros((8, W), f32)

    def fwd_body(j, carry):
        carry = chunk(j, carry, 0)
        y_ref[pl.ds(pl.multiple_of(j * SCAN_CHUNK, SCAN_CHUNK), SCAN_CHUNK), :] = hbuf[...]
        return carry

    lax.fori_loop(0, n_chunks, fwd_body, zero)

    def bwd_body(k, carry):
        j = jnp.where(k == 0, 0, n_chunks - k)
        carry = chunk(j, carry, 1)
        rows = pl.ds(pl.multiple_of(j * SCAN_CHUNK, SCAN_CHUNK), SCAN_CHUNK)
        y_ref[rows, :] = (y_ref[rows, :] + hbuf[...]) * _gelu_tanh(p_ref[rows, W:2 * W])
        return carry

    lax.fori_loop(0, n_chunks, bwd_body, zero)


def _lru(proj, cw, cb, wbd, gb, csp, n_ctx):
    B, T, _ = proj.shape
    W = LRU_WIDTH
    full = lambda *s: pl.BlockSpec(s, lambda b: (0,) * len(s))
    return pl.pallas_call(
        functools.partial(_lru_kernel, n_ctx=n_ctx),
        grid=(B,),
        in_specs=[pl.BlockSpec((None, T, 2 * W), lambda b: (b, 0, LRU_BLK)),
                  full(CONV_WIDTH, W), full(1, W), full(W, 4 * W), full(1, 4 * W), full(2, W)],
        out_specs=pl.BlockSpec((None, T, W), lambda b: (b, 0, 0)),
        out_shape=jax.ShapeDtypeStruct((B, T, W), f32),
        scratch_shapes=[pltpu.VMEM((T + 3 * SEG_PAD, W), f32), pltpu.VMEM((SCAN_CHUNK, W), f32),
                        pltpu.VMEM((SCAN_CHUNK, W), f32), pltpu.VMEM((SCAN_CHUNK, W), f32)],
        compiler_params=_cparams(1), name="rglru",
    )(proj, cw, cb, wbd, gb, csp)


def _attn_kernel(sink_ref, q_ref, kv_ref, cos_ref, sin_ref, y_ref, qs, kp, vp, kc, vc, *, n_ctx):
    T = q_ref.shape[0]
    L = T - n_ctx
    hd = ATTN_HEAD_DIM
    G, R = ATTN_KV_HEADS, ATTN_REP
    nb = L // ATTN_BLOCK
    span = 3 * ATTN_BLOCK
    scale = np.float32(hd ** -0.5)

    def swap_halves(x):
        lane = lax.broadcasted_iota(jnp.int32, x.shape, 1)
        return jnp.where((lane & 31) < 16, pltpu.roll(x, 128 - 16, 1), pltpu.roll(x, 16, 1))

    zpad = jnp.zeros((ATTN_BLOCK, hd), bf16)
    for g in range(G):
        kp[g, 0:ATTN_BLOCK, :] = zpad
        kp[g, ATTN_BLOCK + L:2 * ATTN_BLOCK + L, :] = zpad
        vp[g, 0:ATTN_BLOCK, :] = zpad
        vp[g, ATTN_BLOCK + L:2 * ATTN_BLOCK + L, :] = zpad
        kc[g] = kv_ref[0:n_ctx, hd * g:hd * g + hd].astype(bf16)
        vc[g] = kv_ref[0:n_ctx, hd * (G + g):hd * (G + g) + hd].astype(bf16)

    def rope_rows(c, carry):
        r0 = pl.multiple_of(c * ATTN_BLOCK, ATTN_BLOCK)
        cos = cos_ref[pl.ds(r0, ATTN_BLOCK), :]
        sin = sin_ref[pl.ds(r0, ATTN_BLOCK), :]
        for s in range(ATTN_HEADS * hd // 128):
            xq = q_ref[pl.ds(n_ctx + r0, ATTN_BLOCK), 128 * s:128 * s + 128]
            xr = ((xq * cos + swap_halves(xq) * sin) * scale).astype(bf16)
            qs[2 * s, pl.ds(r0, ATTN_BLOCK), :] = xr[:, 0:hd]
            qs[2 * s + 1, pl.ds(r0, ATTN_BLOCK), :] = xr[:, hd:2 * hd]
        xk = kv_ref[pl.ds(n_ctx + r0, ATTN_BLOCK), 0:128]
        xr = (xk * cos + swap_halves(xk) * sin).astype(bf16)
        xv = kv_ref[pl.ds(n_ctx + r0, ATTN_BLOCK), 128:256].astype(bf16)
        for g in range(G):
            kp[g, pl.ds(ATTN_BLOCK + r0, ATTN_BLOCK), :] = xr[:, hd * g:hd * g + hd]
            vp[g, pl.ds(ATTN_BLOCK + r0, ATTN_BLOCK), :] = xv[:, hd * g:hd * g + hd]
        return carry

    lax.fori_loop(0, nb, rope_rows, 0)

    def sink_column(g, rows_per_head):
        rows = lax.broadcasted_iota(jnp.int32, (R * rows_per_head, 1), 0)
        col = jnp.full((R * rows_per_head, 1), sink_ref[R * g + R - 1], f32)
        for r in range(R - 2, -1, -1):
            col = jnp.where(rows < (r + 1) * rows_per_head, sink_ref[R * g + r], col)
        return col

    def softmax_pv(pieces, sink, v):
        blocks = [s[:, 128 * j:128 * (j + 1)] for s in pieces for j in range(s.shape[1] // 128)]
        m = functools.reduce(jnp.maximum, blocks)
        m = jnp.maximum(jnp.max(m, axis=-1, keepdims=True), sink)
        ps = [jnp.exp(s - m) for s in blocks]
        den = jnp.sum(functools.reduce(jnp.add, ps), axis=-1, keepdims=True) + jnp.exp(sink - m)
        return _dot(jnp.concatenate([p.astype(bf16) for p in ps], axis=1), v) * (1.0 / den)

    qi = lax.broadcasted_iota(jnp.int32, (R * ATTN_BLOCK, ATTN_BLOCK), 0) & (ATTN_BLOCK - 1)
    kj = lax.broadcasted_iota(jnp.int32, (R * ATTN_BLOCK, ATTN_BLOCK), 1)
    bias_a = jnp.where(kj - qi >= ATTN_BLOCK - ATTN_WINDOW, 0.0, MASK_BIAS).astype(f32)
    bias_c = jnp.where(kj - qi <= ATTN_WINDOW - ATTN_BLOCK, 0.0, MASK_BIAS).astype(f32)

    def block(n, carry):
        r0 = pl.multiple_of(n * ATTN_BLOCK, ATTN_BLOCK)
        ba = bias_a + jnp.where(n == 0, MASK_BIAS, 0.0)
        bc = bias_c + jnp.where(n == nb - 1, MASK_BIAS, 0.0)
        for g in range(G):
            q = jnp.concatenate([qs[R * g + r, pl.ds(r0, ATTN_BLOCK), :] for r in range(R)], axis=0)
            k = jnp.concatenate([kp[g, pl.ds(r0, span), :], kc[g]], axis=0)
            v = jnp.concatenate([vp[g, pl.ds(r0, span), :], vc[g]], axis=0)
            s = _dot_nt(q, k)
            pieces = [s[:, 0:ATTN_BLOCK] + ba, s[:, ATTN_BLOCK:2 * ATTN_BLOCK],
                      s[:, 2 * ATTN_BLOCK:span] + bc, s[:, span:]]
            o = softmax_pv(pieces, sink_column(g, ATTN_BLOCK), v)
            y_ref[pl.ds(n_ctx + r0, ATTN_BLOCK), R * hd * g:R * hd * (g + 1)] = jnp.concatenate(
                [o[ATTN_BLOCK * r:ATTN_BLOCK * (r + 1)] for r in range(R)], axis=1)
        return carry

    lax.fori_loop(0, nb, block, 0)

    for g in range(G):
        q = jnp.concatenate(
            [(q_ref[0:n_ctx, hd * (R * g + r):hd * (R * g + r + 1)] * scale).astype(bf16) for r in range(R)], axis=0)
        o = softmax_pv([_dot_nt(q, kc[g])], sink_column(g, n_ctx), vc[g])
        y_ref[0:n_ctx, R * hd * g:R * hd * (g + 1)] = jnp.concatenate(
            [o[n_ctx * r:n_ctx * (r + 1)] for r in range(R)], axis=1)


def _attn(proj, sink, cos_t, sin_t, n_ctx):
    B, T, _ = proj.shape
    L = T - n_ctx
    W = BRANCH_WIDTH
    hd = ATTN_HEAD_DIM
    return pl.pallas_call(
        functools.partial(_attn_kernel, n_ctx=n_ctx),
        grid=(B,),
        in_specs=[pl.BlockSpec(memory_space=pltpu.SMEM),
                  pl.BlockSpec((None, T, W), lambda b: (b, 0, ATTN_Q_BLK)),
                  pl.BlockSpec((None, T, 256), lambda b: (b, 0, ATTN_KV_BLK)),
                  pl.BlockSpec((L, 128), lambda b: (0, 0)),
                  pl.BlockSpec((L, 128), lambda b: (0, 0))],
        out_specs=pl.BlockSpec((None, T, W), lambda b: (b, 0, 0)),
        out_shape=jax.ShapeDtypeStruct((B, T, W), f32),
        scratch_shapes=[pltpu.VMEM((ATTN_HEADS, L, hd), bf16),
                        pltpu.VMEM((ATTN_KV_HEADS, L + 2 * ATTN_BLOCK, hd), bf16),
                        pltpu.VMEM((ATTN_KV_HEADS, L + 2 * ATTN_BLOCK, hd), bf16),
                        pltpu.VMEM((ATTN_KV_HEADS, n_ctx, hd), bf16),
                        pltpu.VMEM((ATTN_KV_HEADS, n_ctx, hd), bf16)],
        compiler_params=_cparams(1), name="window_attn",
    )(sink, proj, proj, cos_t, sin_t)


def _hgrn_kernel(q_ref, i_ref, ff_ref, fb_ref, g_ref, llb_ref, lrest_ref, gain_ref, y_ref,
                 qv, kk0, kk1, cum0, cum1, yf, yb, qs0, qs1, kd0, kd1, et0, et1, st0, st1, st2, st3, *, n_ctx):
    T = q_ref.shape[0]
    K = HGRN_HEAD_DIM
    HP = 2
    TILE = ROW_TILE
    CPT = TILE // CHUNK
    n_tiles = T // TILE
    n_chunks = T // CHUNK
    n_ctx_chunks = n_ctx // CHUNK
    kk_refs, cum_refs, y_refs = (kk0, kk1), (cum0, cum1), (yf, yb)
    qs_refs, kd_refs, et_refs = (qs0, qs1), (kd0, kd1), (et0, et1)
    st_refs = (st0, st1, st2, st3)

    def chunk_rows(a):
        return a.reshape(CPT, CHUNK, a.shape[-1])

    def chunk_total(cum, d):
        c4 = chunk_rows(cum)
        return c4[:, 0:1, :] if d == 1 else c4[:, CHUNK - 1:CHUNK, :]

    rt = lax.broadcasted_iota(jnp.int32, (TILE, TILE), 0)
    ct = lax.broadcasted_iota(jnp.int32, (TILE, TILE), 1)
    same_chunk = (rt // CHUNK) == (ct // CHUNK)
    pair_ok = {0: same_chunk & (ct <= rt), 1: same_chunk & (ct >= rt)}
    pair_bf = {d: pair_ok[d].astype(bf16) for d in (0, 1)}

    llb = llb_ref[...]
    lrest = lrest_ref[...]
    one_minus_lb = jnp.exp(lrest)

    def gates(j, min_tot):
        rows = pl.ds(pl.multiple_of(j * TILE, TILE), TILE)
        qv[rows, :] = _silu(q_ref[rows, :])
        for d, z_ref in enumerate((ff_ref, fb_ref)):
            z = z_ref[rows, :]
            lf = _logaddexp(llb, lrest + _log_sigmoid(z))
            kk_refs[d][rows, :] = one_minus_lb * _sigmoid(-z)
            cum = _split_dot(pair_bf[d], lf)
            cum_refs[d][rows, :] = cum
            min_tot = jnp.minimum(min_tot, jnp.min(chunk_total(cum, d), axis=0))
        return min_tot

    min_tot = lax.fori_loop(0, n_tiles, gates, jnp.zeros((1, HP * K), f32))
    fast_ok = jnp.min(min_tot) > HGRN_FAST_MIN_TOTAL
    for st in st_refs:
        st[...] = jnp.zeros(st.shape, f32)

    def chunk_of(k, d):
        if d == 0:
            return k
        return jnp.where(k < n_ctx_chunks, n_ctx_chunks - 1 - k, n_chunks + n_ctx_chunks - 1 - k)

    def intra(j, carry):
        rows = pl.ds(pl.multiple_of(j * TILE, TILE), TILE)
        for d in (0, 1):
            cum2 = cum_refs[d][rows, :]
            tot4 = chunk_total(cum2, d)
            et_refs[d][pl.ds(pl.multiple_of(j * CPT, CPT), CPT), :, :] = jnp.broadcast_to(
                jnp.exp(tot4), (CPT, 8, HP * K))
            r2 = jnp.broadcast_to(0.5 * tot4, (CPT, CHUNK, HP * K)).reshape(TILE, HP * K)
            e1 = jnp.exp(cum2 - r2)
            e2 = jnp.exp(r2 - cum2)
            er = jnp.exp(r2)
            qa = qv[rows, :] * e1
            kb = kk_refs[d][rows, :] * e2
            qs_refs[d][rows, :] = (qa * er).astype(bf16)
            kd_refs[d][rows, :] = (kb * er).astype(bf16)
            qa = qa.astype(bf16)
            kb = kb.astype(bf16)
            for h in range(HP):
                lanes = slice(K * h, K * (h + 1))
                a = jnp.where(pair_ok[d], _dot_nt(qa[:, lanes], kb[:, lanes]), 0.0)
                y_refs[d][rows, lanes] = _dot(a.astype(bf16), i_ref[rows, lanes].astype(bf16))
        return carry

    def carry_state(k, carry):
        for d in (0, 1):
            c = chunk_of(k, d)
            rows = pl.ds(pl.multiple_of(c * CHUNK, CHUNK), CHUNK)
            et = et_refs[d][c]
            for h in range(HP):
                lanes = slice(K * h, K * (h + 1))
                st = st_refs[2 * h + d]
                s_old = st[...]
                y_refs[d][rows, lanes] += _dot_nt(qs_refs[d][rows, lanes], s_old.astype(bf16))
                st[...] = s_old * et[0:1, lanes] + _dot_tn(i_ref[rows, lanes].astype(bf16), kd_refs[d][rows, lanes])
        return carry

    @pl.when(fast_ok)
    def _():
        lax.fori_loop(0, n_tiles, intra, 0)
        lax.fori_loop(0, n_chunks, carry_state, 0, unroll=HGRN_UNROLL)

    tril = {d: _tri(CHUNK, d == 1) for d in (0, 1)}
    rows_k = lax.broadcasted_iota(jnp.int32, (CHUNK, K), 0)
    cols_c = lax.broadcasted_iota(jnp.int32, (CHUNK, CHUNK), 1)

    def exact_scores(q, kk, cum):
        def col(s, acc):
            sel = rows_k == s
            cs = jnp.sum(jnp.where(sel, cum, 0.0), axis=0, keepdims=True)
            ks = jnp.sum(jnp.where(sel, kk, 0.0), axis=0, keepdims=True)
            w = jnp.exp(jnp.minimum(cum - cs, 0.0))
            return jnp.where(cols_c == s, jnp.sum(q * ks * w, axis=1, keepdims=True), acc)
        return lax.fori_loop(0, CHUNK, col, jnp.zeros((CHUNK, CHUNK), f32))

    def exact_step(k, carry):
        for d in (0, 1):
            c = chunk_of(k, d)
            rows = pl.ds(pl.multiple_of(c * CHUNK, CHUNK), CHUNK)
            for h in range(HP):
                lanes = slice(K * h, K * (h + 1))
                cum = cum_refs[d][rows, lanes]
                kk = kk_refs[d][rows, lanes]
                q = qv[rows, lanes]
                v = i_ref[rows, lanes].astype(bf16)
                tot = cum[0:1, :] if d == 1 else cum[CHUNK - 1:CHUNK, :]
                a = jnp.where(tril[d], exact_scores(q, kk, cum), 0.0)
                st = st_refs[2 * h + d]
                s_old = st[...]
                y_refs[d][rows, lanes] = _dot(a.astype(bf16), v) + _dot_nt(
                    (q * jnp.exp(cum)).astype(bf16), s_old.astype(bf16))
                st[...] = s_old * jnp.exp(tot) + _dot_tn(v, (kk * jnp.exp(tot - cum)).astype(bf16))
        return carry

    @pl.when(jnp.logical_not(fast_ok))
    def _():
        lax.fori_loop(0, n_chunks, exact_step, 0)

    gain = gain_ref[...]

    def readout(c, carry):
        rows = pl.ds(pl.multiple_of(c * ROW_TILE, ROW_TILE), ROW_TILE)
        o = yf[rows, :] + yb[rows, :]
        gate = _silu(g_ref[rows, :])
        y_ref[rows, :] = jnp.concatenate(
            [_rms(o[:, K * h:K * (h + 1)], gain[:, K * h:K * (h + 1)]) for h in range(HP)], axis=1) * gate
        return carry

    lax.fori_loop(0, T // ROW_TILE, readout, 0)


def _hgrn(proj, log_lb, log_rest, gain, n_ctx):
    B, T, _ = proj.shape
    wblk = 2 * HGRN_HEAD_DIM
    col = lambda base: pl.BlockSpec((None, T, wblk), lambda b, p: (b, 0, base + p))
    par = pl.BlockSpec((1, wblk), lambda b, p: (0, p))
    K = HGRN_HEAD_DIM
    return pl.pallas_call(
        functools.partial(_hgrn_kernel, n_ctx=n_ctx),
        grid=(B, HGRN_HEADS // 2),
        in_specs=[col(b0) for b0 in HGRN_BLK] + [par, par, par],
        out_specs=pl.BlockSpec((None, T, wblk), lambda b, p: (b, 0, p)),
        out_shape=jax.ShapeDtypeStruct((B, T, BRANCH_WIDTH), f32),
        scratch_shapes=([pltpu.VMEM((T, wblk), f32)] * 7 + [pltpu.VMEM((T, wblk), bf16)] * 4
                        + [pltpu.VMEM((T // CHUNK, 8, wblk), f32)] * 2 + [pltpu.VMEM((K, K), f32)] * 4),
        compiler_params=_cparams(2), name="hgrn2",
    )(proj, proj, proj, proj, proj, log_lb, log_rest, gain)


def _ssd_kernel(z_ref, x_ref, b_ref, c_ref, dt_ref, cwx_ref, cbx_ref, cwb_ref, cbb_ref, cwc_ref, cbc_ref,
                dtbias_ref, arow_ref, skip_ref, gain_ref, y_ref,
                xstage, bstage, cstage, xs, bs, cs, xd0, xd1, cum0, cum1, dec0, dec1, yf, yb, st0, st1, *, n_ctx):
    T = x_ref.shape[0]
    P = SSD_HEAD_DIM
    R = SSD_HPG
    XW = R * P
    TILE = SCAN_CHUNK
    n_chunks = T // CHUNK
    n_ctx_chunks = n_ctx // CHUNK
    y_refs = (yf, yb)
    st_refs = (st0, st1)
    xd_refs, cum_refs, dec_refs = (xd0, xd1), (cum0, cum1), (dec0, dec1)

    rt = lax.broadcasted_iota(jnp.int32, (TILE, TILE), 0)
    ct = lax.broadcasted_iota(jnp.int32, (TILE, TILE), 1)
    same_chunk = (rt // CHUNK) == (ct // CHUNK)
    pair_bf = {0: (same_chunk & (ct <= rt)).astype(bf16), 1: (same_chunk & (ct >= rt)).astype(bf16)}
    r4 = lax.broadcasted_iota(jnp.int32, (TILE, XW), 0) % CHUNK
    c4 = lax.broadcasted_iota(jnp.int32, (TILE, XW), 1) % CHUNK
    incl4 = {0: c4 <= r4, 1: c4 >= r4}
    strict4 = {0: r4 > c4, 1: r4 < c4}

    _stage_segments(xstage, x_ref, 0, XW, n_ctx, T)
    _stage_segments(bstage, b_ref, 0, SSD_STATE, n_ctx, T)
    _stage_segments(cstage, c_ref, 0, SSD_STATE, n_ctx, T)
    cwx, cbx = cwx_ref[...], cbx_ref[...]
    cwb, cbb = cwb_ref[...], cbb_ref[...]
    cwc, cbc = cwc_ref[...], cbc_ref[...]
    dtbias, arow = dtbias_ref[...], arow_ref[...]
    src = lax.broadcasted_iota(jnp.int32, (128, 2 * XW), 0)
    dst = lax.broadcasted_iota(jnp.int32, (128, 2 * XW), 1)
    spread = (src == R * (dst // XW) + (dst % XW) // P).astype(bf16)

    def prologue(j, carry):
        off = _staged_window(j)
        rows = pl.ds(pl.multiple_of(j * SCAN_CHUNK, SCAN_CHUNK), SCAN_CHUNK)
        x = _silu(_conv_chunk(xstage, off, cwx, cbx))
        xs[rows, :] = x
        bs[rows, :] = _silu(_conv_chunk(bstage, off, cwb, cbb)).astype(bf16)
        cs[rows, :] = _silu(_conv_chunk(cstage, off, cwc, cbc)).astype(bf16)
        dt_all = _split_dot_right(_softplus(dt_ref[rows, :] + dtbias), spread)
        for d in (0, 1):
            dt = dt_all[:, XW * d:XW * (d + 1)]
            da = dt * arow[:, XW * d:XW * (d + 1)]
            xd_refs[d][rows, :] = x * dt
            cum_refs[d][rows, :] = _split_dot(pair_bf[d], da)
            logdec = _split_dot(pair_bf[d], jnp.where(strict4[d], da, 0.0))
            dec_refs[d][rows, :] = jnp.where(incl4[d], jnp.exp(logdec), 0.0)
        return carry

    lax.fori_loop(0, T // SCAN_CHUNK, prologue, 0)
    for st in st_refs:
        st[...] = jnp.zeros(st.shape, f32)

    def chunk_of(k, d):
        if d == 0:
            return k
        return jnp.where(k < n_ctx_chunks, n_ctx_chunks - 1 - k, n_chunks + n_ctx_chunks - 1 - k)

    same_head = (lax.broadcasted_iota(jnp.int32, (XW, XW), 0) // P
                 == lax.broadcasted_iota(jnp.int32, (XW, XW), 1) // P)

    def step(k, d):
        c = chunk_of(k, d)
        rows = pl.ds(pl.multiple_of(c * CHUNK, CHUNK), CHUNK)
        cm = cs[rows, :]
        bm = bs[rows, :]
        xdt = xd_refs[d][rows, :]
        cum = cum_refs[d][rows, :]
        tot = cum[0:1, :] if d == 1 else cum[CHUNK - 1:CHUNK, :]
        cb = _dot_nt(cm, jnp.concatenate([bm] * R, axis=0))
        x_heads = jnp.where(same_head, jnp.concatenate([xdt] * R, axis=0), 0.0)
        intra = _dot((cb * dec_refs[d][rows, :]).astype(bf16), x_heads.astype(bf16))
        st = st_refs[d]
        s_old = st[...]
        inter = _dot(cm, s_old.astype(bf16)) * jnp.exp(cum)
        y_refs[d][rows, :] = intra + inter
        st[...] = s_old * jnp.exp(tot) + _dot_tn(bm, (xdt * jnp.exp(tot - cum)).astype(bf16))

    def body(k, carry):
        step(k, 0)
        step(k, 1)
        return carry

    lax.fori_loop(0, n_chunks, body, 0, unroll=SSD_UNROLL)

    skip, gain = skip_ref[...], gain_ref[...]

    def readout(c, carry):
        rows = pl.ds(pl.multiple_of(c * ROW_TILE, ROW_TILE), ROW_TILE)
        y = skip * xs[rows, :] + yf[rows, :] + yb[rows, :]
        y_ref[rows, :] = _rms(y * _silu(z_ref[rows, :]), gain)
        return carry

    lax.fori_loop(0, T // ROW_TILE, readout, 0)


def _ssd(proj, cwx, cbx, cwb, cbb, cwc, cbc, dtbias, arow, skip, gain, n_ctx):
    B, T, _ = proj.shape
    XW = SSD_HPG * SSD_HEAD_DIM
    N = SSD_STATE
    col = lambda base, w: pl.BlockSpec((None, T, w), lambda b, g: (b, 0, base + g))
    par = lambda r, w: pl.BlockSpec((None, r, w), lambda b, g: (g, 0, 0))
    return pl.pallas_call(
        functools.partial(_ssd_kernel, n_ctx=n_ctx),
        grid=(B, SSD_GROUPS),
        in_specs=[col(SSD_Z_BLK, XW), col(SSD_X_BLK, XW), col(SSD_B_BLK, N), col(SSD_C_BLK, N), col(SSD_DT_BLK, 128),
                  par(CONV_WIDTH, XW), par(1, XW), par(CONV_WIDTH, N), par(1, N), par(CONV_WIDTH, N), par(1, N),
                  par(1, 128), par(1, 2 * XW), par(1, XW), par(1, XW)],
        out_specs=pl.BlockSpec((None, T, XW), lambda b, g: (b, 0, g)),
        out_shape=jax.ShapeDtypeStruct((B, T, SSD_WIDTH), f32),
        scratch_shapes=[pltpu.VMEM((T + 3 * SEG_PAD, XW), f32), pltpu.VMEM((T + 3 * SEG_PAD, N), f32),
                        pltpu.VMEM((T + 3 * SEG_PAD, N), f32),
                        pltpu.VMEM((T, XW), f32), pltpu.VMEM((T, N), bf16), pltpu.VMEM((T, N), bf16),
                        pltpu.VMEM((T, XW), f32), pltpu.VMEM((T, XW), f32), pltpu.VMEM((T, XW), f32),
                        pltpu.VMEM((T, XW), f32), pltpu.VMEM((T, XW), f32), pltpu.VMEM((T, XW), f32),
                        pltpu.VMEM((T, XW), f32), pltpu.VMEM((T, XW), f32),
                        pltpu.VMEM((N, XW), f32), pltpu.VMEM((N, XW), f32)],
        compiler_params=_cparams(2), name="ssd",
    )(proj, proj, proj, proj, proj, cwx, cbx, cwb, cbb, cwc, cbc, dtbias, arow, skip, gain)


def _post_kernel(x_ref, gl_ref, ya_ref, yb_ref, yc_ref, yd_ref, m_ref, wbr_ref, wout_ref, wup_ref, wdn_ref,
                 g1_ref, g2_ref, g3_ref, o_ref):
    D = D_MODEL
    ys = (ya_ref, yb_ref, yc_ref, yd_ref)
    merged = None
    for i in range(N_BRANCH):
        t = _sigmoid(gl_ref[:, D * i:D * (i + 1)]) * _dot(ys[i][...].astype(bf16), wbr_ref[i])
        merged = t if merged is None else merged + t
    mix = _dot(merged.astype(bf16), wout_ref[...])
    mod = lambda k: m_ref[:, D * k:D * (k + 1)]
    x1 = x_ref[...] + mod(2) * _rms(mix, g1_ref[...])
    h2 = (_rms(x1, g2_ref[...]) * (1.0 + mod(4)) + mod(3)).astype(bf16)
    down = None
    hc = 1024
    for c in range(MLP_HIDDEN // hc):
        u = jnp.maximum(_dot(h2, wup_ref[:, hc * c:hc * (c + 1)]), 0.0)
        t = _dot((u * u).astype(bf16), wdn_ref[hc * c:hc * (c + 1), :])
        down = t if down is None else down + t
    o_ref[...] = x1 + mod(5) * _rms(down, g3_ref[...])


def _post(xs, proj, ya, yb, yc, yd, mods_l, wbr, wout, wup, wdn, g1, g2, g3, skip_ctx):
    B, T, D = xs.shape
    W = BRANCH_WIDTH
    nt = T // ROW_TILE
    first = 1 if skip_ctx else 0
    tile = lambda w: pl.BlockSpec((None, ROW_TILE, w), lambda b, i: (b, i + first, 0))
    once = lambda *s: pl.BlockSpec(s, lambda b, i: (0,) * len(s), pipeline_mode=pl.Buffered(1))
    mod_idx = _mod_row_index(B)
    return pl.pallas_call(
        _post_kernel,
        grid=(B, nt - first),
        in_specs=[tile(D), tile(MERGE_COLS), tile(W), tile(W), tile(W), tile(W),
                  pl.BlockSpec((None, 1, 6 * D), lambda b, i: mod_idx(b, i + first)),
                  once(N_BRANCH, W, D), once(D, D), once(D, MLP_HIDDEN), once(MLP_HIDDEN, D),
                  once(1, D), once(1, D), once(1, D)],
        out_specs=pl.BlockSpec((None, ROW_TILE, D), lambda b, i: (b, i, 0)),
        out_shape=jax.ShapeDtypeStruct((B, T - first * ROW_TILE, D), f32),
        compiler_params=_cparams(2), name="merge_mlp",
    )(xs, proj, ya, yb, yc, yd, mods_l, wbr, wout, wup, wdn, g1, g2, g3)


def _block_diag(w):
    eye = jnp.eye(LRU_BLOCKS, dtype=w.dtype)
    return jnp.einsum('hij,hg->higj', w, eye).reshape(LRU_WIDTH, LRU_WIDTH)


def _rope_tables(n_lat):
    half = ATTN_HEAD_DIM // 2
    quarter = half // 2
    inv_freq = ROPE_BASE ** (-jnp.arange(quarter, dtype=f32) / quarter)
    t = jnp.arange(n_lat, dtype=jnp.int32)
    rows = (t // GRID_W).astype(f32)[:, None] * inv_freq
    cols = (t % GRID_W).astype(f32)[:, None] * inv_freq
    cos_h = jnp.concatenate([jnp.cos(rows), jnp.cos(rows), jnp.cos(cols), jnp.cos(cols)], axis=-1)
    sin_h = jnp.concatenate([-jnp.sin(rows), jnp.sin(rows), -jnp.sin(cols), jnp.sin(cols)], axis=-1)
    return jnp.tile(cos_h, (1, 2)), jnp.tile(sin_h, (1, 2))


def kernel(x, c, ctx, c_ctx, w_ada, b_ada, g_pre_mix, g_post_mix, g_pre_mlp, g_post_mlp, w_in, lru_conv_w, lru_conv_b, lru_rec_w, lru_rec_b, lru_inp_w, lru_inp_b, lru_lambda, attn_sink, hgrn_lb_logits, hgrn_norm_g, ssd_conv_w, ssd_conv_b, ssd_dt_bias, ssd_a_log, ssd_skip, ssd_norm_g, w_branch, w_out, w_mlp_up, w_mlp_down):
    B, L, D = x.shape
    n_ctx = ctx.shape[1]
    assert D == D_MODEL and n_ctx == ROW_TILE == SCAN_CHUNK and L % ROW_TILE == 0 and L % GRID_W == 0
    assert w_in.shape == (DEPTH, D_MODEL, IN_COLS)

    xs = jnp.concatenate([ctx, x], axis=1)
    pad_rows = (-(B + 1)) % 8
    c_all = jnp.concatenate([c, c_ctx[None, :], jnp.zeros((pad_rows, D), f32)], axis=0)
    mods = _ada(c_all, w_ada, b_ada)
    mods = mods.reshape(DEPTH, c_all.shape[0], 1, 6 * D)

    lb = jnp.cumsum(jax.nn.softmax(hgrn_lb_logits.astype(f32), axis=0), axis=0)
    lb = lb - lb[0]
    cos_t, sin_t = _rope_tables(L)
    G, R, P, N = SSD_GROUPS, SSD_HPG, SSD_HEAD_DIM, SSD_STATE

    for l in range(DEPTH):
        last = l == DEPTH - 1
        wl = w_in[l]
        dtf, dtb = wl[:, 5888:5896], wl[:, 5896:5904]
        zpad = jnp.zeros((D, 128 - 2 * R), f32)
        dt_cols = [jnp.concatenate([dtf[:, R * g:R * (g + 1)], dtb[:, R * g:R * (g + 1)], zpad], axis=1)
                   for g in range(G)]
        w_perm = jnp.concatenate([wl[:, 5904:], wl[:, :5888]] + dt_cols, axis=1).astype(bf16)

        h = _norm_mod(xs, mods[l], g_pre_mix[l][None, :])
        proj = _in_proj(h, w_perm)

        wbd = jnp.concatenate([_block_diag(lru_rec_w[l, 0]), _block_diag(lru_inp_w[l, 0]),
                               _block_diag(lru_rec_w[l, 1]), _block_diag(lru_inp_w[l, 1])], axis=1).astype(bf16)
        gb = jnp.concatenate([lru_rec_b[l, 0], lru_inp_b[l, 0], lru_rec_b[l, 1], lru_inp_b[l, 1]])[None, :]
        csp = LRU_C * jax.nn.softplus(-lru_lambda[l])
        ya = _lru(proj, lru_conv_w[l], lru_conv_b[l][None, :], wbd, gb, csp, n_ctx)

        yb = _attn(proj, attn_sink[l], cos_t, sin_t, n_ctx)

        lbl = lb[l][None, :]
        yc = _hgrn(proj, jnp.log(lbl), jnp.log1p(-lbl), hgrn_norm_g[l][None, :], n_ctx)

        cw, cbv = ssd_conv_w[l], ssd_conv_b[l]
        XW = R * P
        grp = lambda a, off, w: jnp.stack([a[..., off + w * g:off + w * (g + 1)] for g in range(G)], axis=0)
        cwx, cbx = grp(cw, 0, XW), grp(cbv[None, :], 0, XW)
        cwb, cbb = grp(cw, SSD_WIDTH, N), grp(cbv[None, :], SSD_WIDTH, N)
        cwc, cbc = grp(cw, SSD_WIDTH + G * N, N), grp(cbv[None, :], SSD_WIDTH + G * N, N)
        a_neg = -jnp.exp(ssd_a_log[l].astype(f32))
        per_group = lambda v: [jnp.concatenate([v[0, R * g:R * (g + 1)], v[1, R * g:R * (g + 1)]]) for g in range(G)]
        lanes_compact = lambda v: jnp.stack([jnp.pad(u, (0, 128 - 2 * R))[None, :] for u in per_group(v)], axis=0)
        lanes_spread = lambda v: jnp.stack([jnp.repeat(u, P)[None, :] for u in per_group(v)], axis=0)
        skip = jnp.repeat(ssd_skip[l], P).reshape(G, 1, XW)
        gain_d = ssd_norm_g[l].reshape(G, 1, XW)
        yd = _ssd(proj, cwx, cbx, cwb, cbb, cwc, cbc, lanes_compact(ssd_dt_bias[l]), lanes_spread(a_neg), skip, gain_d, n_ctx)

        xs = _post(xs, proj, ya, yb, yc, yd, mods[l], w_branch[l].astype(bf16), w_out[l].astype(bf16),
                   w_mlp_up[l].astype(bf16), w_mlp_down[l].astype(bf16),
                   g_post_mix[l][None, :], g_pre_mlp[l][None, :], g_post_mlp[l][None, :], skip_ctx=last)
    return xs
```

```python
import functools

import numpy as np
import jax
import jax.numpy as jnp
from jax import lax
from jax.experimental import pallas as pl
from jax.experimental.pallas import tpu as pltpu

f32 = jnp.float32
bf16 = jnp.bfloat16

D_MODEL = 1024
DEPTH = 2
GRID_W = 64
N_BRANCH = 4
BRANCH_WIDTH = D_MODEL // 2
CONV_WIDTH = 4
LRU_WIDTH = BRANCH_WIDTH
LRU_BLOCKS = 8
LRU_BLOCK_W = LRU_WIDTH // LRU_BLOCKS
LRU_C = 8.0
ATTN_HEAD_DIM = 64
ATTN_HEADS = BRANCH_WIDTH // ATTN_HEAD_DIM
ATTN_KV_HEADS = 2
ATTN_REP = ATTN_HEADS // ATTN_KV_HEADS
ATTN_WINDOW = 128
ATTN_BLOCK = 128
ROPE_BASE = 10000.0
HGRN_HEAD_DIM = 128
HGRN_HEADS = BRANCH_WIDTH // HGRN_HEAD_DIM
SSD_WIDTH = BRANCH_WIDTH
SSD_HEAD_DIM = 64
SSD_HEADS = SSD_WIDTH // SSD_HEAD_DIM
SSD_GROUPS = 2
SSD_HPG = SSD_HEADS // SSD_GROUPS
SSD_STATE = 128
MLP_HIDDEN = 4 * D_MODEL
NORM_EPS = 1e-6

IN_COLS = 10000
MERGE_COLS = N_BRANCH * D_MODEL
PROJ_COLS = 10240
LRU_BLK = 4
ATTN_Q_BLK = 10
ATTN_KV_BLK = 22
HGRN_BLK = (23, 25, 27, 29, 31)
SSD_Z_BLK = 33
SSD_X_BLK = 35
SSD_B_BLK = 74
SSD_C_BLK = 76
SSD_DT_BLK = 78

ROW_TILE = 256
SCAN_CHUNK = 256
CHUNK = 64
SEG_PAD = 8
HGRN_FAST_MIN_TOTAL = -160.0
HGRN_UNROLL = 6
SSD_UNROLL = 6
MASK_BIAS = -1e30
VMEM_LIMIT = 58 * 1024 * 1024


def _cparams(n_axes):
    return pltpu.CompilerParams(dimension_semantics=("arbitrary",) * n_axes, vmem_limit_bytes=VMEM_LIMIT)


def _sigmoid(x):
    return 0.5 * jnp.tanh(0.5 * x) + 0.5


def _silu(x):
    return x * jax.nn.sigmoid(x)


def _gelu_tanh(x):
    return 0.5 * x * (1.0 + jnp.tanh(np.float32(np.sqrt(2.0 / np.pi)) * (x + 0.044715 * (x * x * x))))


def _expm1_given_exp(x, ex):
    poly = x * (1.0 + x * (1.0 / 2) * (1.0 + x * (1.0 / 3) * (1.0 + x * (1.0 / 4))))
    return jnp.where(jnp.abs(x) < 0.03, poly, ex - 1.0)


def _log_sigmoid(x):
    return jnp.minimum(x, 0.0) - jnp.log(1.0 + jnp.exp(-jnp.abs(x)))


def _logaddexp(a, b):
    return jnp.maximum(a, b) + jnp.log(1.0 + jnp.exp(-jnp.abs(a - b)))


def _softplus(x):
    return jnp.maximum(x, 0.0) + jnp.log1p(jnp.exp(-jnp.abs(x)))


def _rms(x, g):
    return x * lax.rsqrt(jnp.mean(x * x, axis=-1, keepdims=True) + NORM_EPS) * g


def _dot(a, b):
    return jnp.dot(a, b, preferred_element_type=f32)


def _dot_nt(a, b):
    return lax.dot_general(a, b, (((1,), (1,)), ((), ())), preferred_element_type=f32)


def _dot_tn(a, b):
    return lax.dot_general(a, b, (((0,), (0,)), ((), ())), preferred_element_type=f32)


def _split_dot(tri, x):
    hi = x.astype(bf16)
    lo = (x - hi.astype(f32)).astype(bf16)
    return _dot(tri, hi) + _dot(tri, lo)


def _split_dot_right(x, sel):
    hi = x.astype(bf16)
    lo = (x - hi.astype(f32)).astype(bf16)
    return _dot(hi, sel) + _dot(lo, sel)


def _tri(n, reverse):
    r = lax.broadcasted_iota(jnp.int32, (n, n), 0)
    c = lax.broadcasted_iota(jnp.int32, (n, n), 1)
    return (c >= r) if reverse else (c <= r)


def _ada_kernel(c_ref, w_ref, b_ref, o_ref):
    cond = _silu(c_ref[...])
    o_ref[...] = jnp.dot(cond, w_ref[...], preferred_element_type=f32,
                         precision=lax.Precision.HIGHEST) + b_ref[...]


def _ada(c_all, w_ada, b_ada):
    rows = c_all.shape[0]
    nt = 6 * D_MODEL // 1024
    return pl.pallas_call(
        _ada_kernel,
        grid=(DEPTH, nt),
        in_specs=[pl.BlockSpec((rows, D_MODEL), lambda l, j: (0, 0)),
                  pl.BlockSpec((None, D_MODEL, 1024), lambda l, j: (l, 0, j)),
                  pl.BlockSpec((None, 1, 1024), lambda l, j: (l, 0, j))],
        out_specs=pl.BlockSpec((None, rows, 1024), lambda l, j: (l, 0, j)),
        out_shape=jax.ShapeDtypeStruct((DEPTH, rows, 6 * D_MODEL), f32),
        compiler_params=_cparams(2), name="ada_mod",
    )(c_all, w_ada, b_ada.reshape(DEPTH, 1, 6 * D_MODEL))


def _mod_row_index(n_batch):
    return lambda b, i: (jnp.where(i == 0, n_batch, b), 0, 0)


def _pre_norm(x, m_ref, g_ref):
    shift = m_ref[:, 0:D_MODEL]
    scale = m_ref[:, D_MODEL:2 * D_MODEL]
    return (_rms(x, g_ref[...]) * (1.0 + scale) + shift).astype(bf16)


def _first_norm_kernel(ctx_ref, x_ref, m_ref, g_ref, xs_ref, h_ref):
    @pl.when(pl.program_id(1) == 0)
    def _():
        xs_ref[...] = ctx_ref[...]

    @pl.when(pl.program_id(1) > 0)
    def _():
        xs_ref[...] = x_ref[...]

    h_ref[...] = _pre_norm(xs_ref[...], m_ref, g_ref)


def _first_norm(ctx, x, mods_l, g):
    B, L, D = x.shape
    T = ctx.shape[1] + L
    tile = pl.BlockSpec((None, ROW_TILE, D), lambda b, i: (b, i, 0))
    return pl.pallas_call(
        _first_norm_kernel,
        grid=(B, T // ROW_TILE),
        in_specs=[pl.BlockSpec((None, ROW_TILE, D), lambda b, i: (b, 0, 0)),
                  pl.BlockSpec((None, ROW_TILE, D), lambda b, i: (b, jnp.maximum(i - 1, 0), 0)),
                  pl.BlockSpec((None, 1, 6 * D), _mod_row_index(B)),
                  pl.BlockSpec((1, D), lambda b, i: (0, 0))],
        out_specs=[tile, tile],
        out_shape=[jax.ShapeDtypeStruct((B, T, D), f32), jax.ShapeDtypeStruct((B, T, D), bf16)],
        compiler_params=_cparams(2), name="first_norm",
    )(ctx, x, mods_l, g)


def _proj_kernel(a_ref, w_ref, o_ref):
    o_ref[...] = _dot(a_ref[...], w_ref[...])


def _in_proj(h, w_perm):
    B, T, _ = h.shape
    tn = 1024
    return pl.pallas_call(
        _proj_kernel,
        grid=(B, PROJ_COLS // tn),
        in_specs=[pl.BlockSpec((None, T, D_MODEL), lambda b, j: (b, 0, 0)),
                  pl.BlockSpec((D_MODEL, tn), lambda b, j: (0, j))],
        out_specs=pl.BlockSpec((None, T, tn), lambda b, j: (b, 0, j)),
        out_shape=jax.ShapeDtypeStruct((B, T, PROJ_COLS), f32),
        compiler_params=_cparams(2), name="in_proj",
    )(h, w_perm)


def _stage_segments(dst, src_ref, col0, width, n_ctx, t_all):
    z = jnp.zeros((SEG_PAD, width), f32)
    dst[0:SEG_PAD, :] = z
    dst[SEG_PAD:SEG_PAD + n_ctx, :] = src_ref[0:n_ctx, col0:col0 + width]
    dst[SEG_PAD + n_ctx:2 * SEG_PAD + n_ctx, :] = z
    dst[2 * SEG_PAD + n_ctx:2 * SEG_PAD + t_all, :] = src_ref[n_ctx:t_all, col0:col0 + width]
    dst[2 * SEG_PAD + t_all:3 * SEG_PAD + t_all, :] = z


def _staged_window(j):
    return pl.multiple_of(SCAN_CHUNK * j + jnp.where(j >= 1, SEG_PAD, 0), SEG_PAD)


def _conv_chunk(stage_ref, win0, cw, cb):
    n = SCAN_CHUNK + 2 * SEG_PAD
    win = stage_ref[pl.ds(win0, n), :]
    acc = cb + cw[2:3, :] * win[SEG_PAD:SEG_PAD + SCAN_CHUNK]
    for tap, o in ((0, -2), (1, -1), (3, 1)):
        acc = acc + cw[tap:tap + 1, :] * pltpu.roll(win, (-o) % n, 0)[SEG_PAD:SEG_PAD + SCAN_CHUNK]
    return acc


def _lru_kernel(p_ref, cw_ref, cb_ref, wbd_ref, gb_ref, csp_ref, y_ref, stage, abuf, bbuf, hbuf, hfwd, *, n_ctx):
    T = p_ref.shape[0]
    W = LRU_WIDTH
    n_chunks = T // SCAN_CHUNK
    n_groups = SCAN_CHUNK // 8
    _stage_segments(stage, p_ref, 0, W, n_ctx, T)
    cw = cw_ref[...]
    cb = cb_ref[...]
    r8 = lax.broadcasted_iota(jnp.int32, (SCAN_CHUNK, W), 0) & 7

    def chunk(j, carry, d):
        reverse = d == 1
        x = _conv_chunk(stage, _staged_window(j), cw, cb)
        g = _dot(x.astype(bf16), wbd_ref[:, 2 * W * d:2 * W * (d + 1)]) + gb_ref[:, 2 * W * d:2 * W * (d + 1)]
        r = _sigmoid(g[:, 0:W])
        i = _sigmoid(g[:, W:2 * W])
        log_a = -csp_ref[d:d + 1, :] * r
        a = jnp.exp(log_a)
        b = jnp.sqrt(-_expm1_given_exp(2.0 * log_a, a * a)) * (i * x)
        for s in (1, 2, 4):
            if reverse:
                m = r8 < 8 - s
                sh = SCAN_CHUNK - s
            else:
                m = r8 >= s
                sh = s
            a_sh = jnp.where(m, pltpu.roll(a, sh, 0), 1.0)
            b_sh = jnp.where(m, pltpu.roll(b, sh, 0), 0.0)
            b = a * b_sh + b
            a = a * a_sh
        abuf[...] = a
        bbuf[...] = b

        def group(gi, c):
            gg = (n_groups - 1 - gi) if reverse else gi
            r0 = pl.multiple_of(gg * 8, 8)
            h = abuf[pl.ds(r0, 8), :] * c + bbuf[pl.ds(r0, 8), :]
            hbuf[pl.ds(r0, 8), :] = h
            edge = h[0:1, :] if reverse else h[7:8, :]
            return jnp.broadcast_to(edge, (8, W))

        return lax.fori_loop(0, n_groups, group, carry, unroll=4)

    zero = jnp.zeros((8, W), f32)

    def fwd_body(j, carry):
        carry = chunk(j, carry, 0)
        hfwd[pl.ds(pl.multiple_of(j * SCAN_CHUNK, SCAN_CHUNK), SCAN_CHUNK), :] = hbuf[...]
        return carry

    lax.fori_loop(0, n_chunks, fwd_body, zero)

    def bwd_body(k, carry):
        j = jnp.where(k == 0, 0, n_chunks - k)
        carry = chunk(j, carry, 1)
        rows = pl.ds(pl.multiple_of(j * SCAN_CHUNK, SCAN_CHUNK), SCAN_CHUNK)
        y_ref[rows, :] = ((hfwd[rows, :] + hbuf[...]) * _gelu_tanh(p_ref[rows, W:2 * W])).astype(y_ref.dtype)
        return carry

    lax.fori_loop(0, n_chunks, bwd_body, zero)


def _lru(proj, cw, cb, wbd, gb, csp, n_ctx):
    B, T, _ = proj.shape
    W = LRU_WIDTH
    full = lambda *s: pl.BlockSpec(s, lambda b: (0,) * len(s))
    return pl.pallas_call(
        functools.partial(_lru_kernel, n_ctx=n_ctx),
        grid=(B,),
        in_specs=[pl.BlockSpec((None, T, 2 * W), lambda b: (b, 0, LRU_BLK)),
                  full(CONV_WIDTH, W), full(1, W), full(W, 4 * W), full(1, 4 * W), full(2, W)],
        out_specs=pl.BlockSpec((None, T, W), lambda b: (b, 0, 0)),
        out_shape=jax.ShapeDtypeStruct((B, T, W), bf16),
        scratch_shapes=[pltpu.VMEM((T + 3 * SEG_PAD, W), f32), pltpu.VMEM((SCAN_CHUNK, W), f32),
                        pltpu.VMEM((SCAN_CHUNK, W), f32), pltpu.VMEM((SCAN_CHUNK, W), f32),
                        pltpu.VMEM((T, W), f32)],
        compiler_params=_cparams(1), name="rglru",
    )(proj, cw, cb, wbd, gb, csp)


def _attn_kernel(sink_ref, q_ref, kv_ref, cos_ref, sin_ref, y_ref, qs, kp, vp, kc, vc, *, n_ctx):
    T = q_ref.shape[0]
    L = T - n_ctx
    hd = ATTN_HEAD_DIM
    G, R = ATTN_KV_HEADS, ATTN_REP
    nb = L // ATTN_BLOCK
    span = 3 * ATTN_BLOCK
    scale = np.float32(hd ** -0.5)

    def swap_halves(x):
        lane = lax.broadcasted_iota(jnp.int32, x.shape, 1)
        return jnp.where((lane & 31) < 16, pltpu.roll(x, 128 - 16, 1), pltpu.roll(x, 16, 1))

    zpad = jnp.zeros((ATTN_BLOCK, hd), bf16)
    for g in range(G):
        kp[g, 0:ATTN_BLOCK, :] = zpad
        kp[g, ATTN_BLOCK + L:2 * ATTN_BLOCK + L, :] = zpad
        vp[g, 0:ATTN_BLOCK, :] = zpad
        vp[g, ATTN_BLOCK + L:2 * ATTN_BLOCK + L, :] = zpad
        kc[g] = kv_ref[0:n_ctx, hd * g:hd * g + hd].astype(bf16)
        vc[g] = kv_ref[0:n_ctx, hd * (G + g):hd * (G + g) + hd].astype(bf16)

    def rope_rows(c, carry):
        r0 = pl.multiple_of(c * ATTN_BLOCK, ATTN_BLOCK)
        cos = cos_ref[pl.ds(r0, ATTN_BLOCK), :]
        sin = sin_ref[pl.ds(r0, ATTN_BLOCK), :]
        for s in range(ATTN_HEADS * hd // 128):
            xq = q_ref[pl.ds(n_ctx + r0, ATTN_BLOCK), 128 * s:128 * s + 128]
            xr = ((xq * cos + swap_halves(xq) * sin) * scale).astype(bf16)
            qs[2 * s, pl.ds(r0, ATTN_BLOCK), :] = xr[:, 0:hd]
            qs[2 * s + 1, pl.ds(r0, ATTN_BLOCK), :] = xr[:, hd:2 * hd]
        xk = kv_ref[pl.ds(n_ctx + r0, ATTN_BLOCK), 0:128]
        xr = (xk * cos + swap_halves(xk) * sin).astype(bf16)
        xv = kv_ref[pl.ds(n_ctx + r0, ATTN_BLOCK), 128:256].astype(bf16)
        for g in range(G):
            kp[g, pl.ds(ATTN_BLOCK + r0, ATTN_BLOCK), :] = xr[:, hd * g:hd * g + hd]
            vp[g, pl.ds(ATTN_BLOCK + r0, ATTN_BLOCK), :] = xv[:, hd * g:hd * g + hd]
        return carry

    lax.fori_loop(0, nb, rope_rows, 0)

    def sink_column(g, rows_per_head):
        rows = lax.broadcasted_iota(jnp.int32, (R * rows_per_head, 1), 0)
        col = jnp.full((R * rows_per_head, 1), sink_ref[R * g + R - 1], f32)
        for r in range(R - 2, -1, -1):
            col = jnp.where(rows < (r + 1) * rows_per_head, sink_ref[R * g + r], col)
        return col

    def softmax_pv(pieces, sink, v):
        blocks = [s[:, 128 * j:128 * (j + 1)] for s in pieces for j in range(s.shape[1] // 128)]
        m = functools.reduce(jnp.maximum, blocks)
        m = jnp.maximum(jnp.max(m, axis=-1, keepdims=True), sink)
        ps = [jnp.exp(s - m) for s in blocks]
        den = jnp.sum(functools.reduce(jnp.add, ps), axis=-1, keepdims=True) + jnp.exp(sink - m)
        return _dot(jnp.concatenate([p.astype(bf16) for p in ps], axis=1), v) * (1.0 / den)

    qi = lax.broadcasted_iota(jnp.int32, (R * ATTN_BLOCK, ATTN_BLOCK), 0) & (ATTN_BLOCK - 1)
    kj = lax.broadcasted_iota(jnp.int32, (R * ATTN_BLOCK, ATTN_BLOCK), 1)
    bias_a = jnp.where(kj - qi >= ATTN_BLOCK - ATTN_WINDOW, 0.0, MASK_BIAS).astype(f32)
    bias_c = jnp.where(kj - qi <= ATTN_WINDOW - ATTN_BLOCK, 0.0, MASK_BIAS).astype(f32)

    def block(n, carry):
        r0 = pl.multiple_of(n * ATTN_BLOCK, ATTN_BLOCK)
        ba = bias_a + jnp.where(n == 0, MASK_BIAS, 0.0)
        bc = bias_c + jnp.where(n == nb - 1, MASK_BIAS, 0.0)
        for g in range(G):
            q = jnp.concatenate([qs[R * g + r, pl.ds(r0, ATTN_BLOCK), :] for r in range(R)], axis=0)
            k = jnp.concatenate([kp[g, pl.ds(r0, span), :], kc[g]], axis=0)
            v = jnp.concatenate([vp[g, pl.ds(r0, span), :], vc[g]], axis=0)
            s = _dot_nt(q, k)
            pieces = [s[:, 0:ATTN_BLOCK] + ba, s[:, ATTN_BLOCK:2 * ATTN_BLOCK],
                      s[:, 2 * ATTN_BLOCK:span] + bc, s[:, span:]]
            o = softmax_pv(pieces, sink_column(g, ATTN_BLOCK), v)
            y_ref[pl.ds(n_ctx + r0, ATTN_BLOCK), R * hd * g:R * hd * (g + 1)] = jnp.concatenate(
                [o[ATTN_BLOCK * r:ATTN_BLOCK * (r + 1)] for r in range(R)], axis=1).astype(y_ref.dtype)
        return carry

    lax.fori_loop(0, nb, block, 0)

    for g in range(G):
        q = jnp.concatenate(
            [(q_ref[0:n_ctx, hd * (R * g + r):hd * (R * g + r + 1)] * scale).astype(bf16) for r in range(R)], axis=0)
        o = softmax_pv([_dot_nt(q, kc[g])], sink_column(g, n_ctx), vc[g])
        y_ref[0:n_ctx, R * hd * g:R * hd * (g + 1)] = jnp.concatenate(
            [o[n_ctx * r:n_ctx * (r + 1)] for r in range(R)], axis=1).astype(y_ref.dtype)


def _attn(proj, sink, cos_t, sin_t, n_ctx):
    B, T, _ = proj.shape
    L = T - n_ctx
    W = BRANCH_WIDTH
    hd = ATTN_HEAD_DIM
    return pl.pallas_call(
        functools.partial(_attn_kernel, n_ctx=n_ctx),
        grid=(B,),
        in_specs=[pl.BlockSpec(memory_space=pltpu.SMEM),
                  pl.BlockSpec((None, T, W), lambda b: (b, 0, ATTN_Q_BLK)),
                  pl.BlockSpec((None, T, 256), lambda b: (b, 0, ATTN_KV_BLK)),
                  pl.BlockSpec((L, 128), lambda b: (0, 0)),
                  pl.BlockSpec((L, 128), lambda b: (0, 0))],
        out_specs=pl.BlockSpec((None, T, W), lambda b: (b, 0, 0)),
        out_shape=jax.ShapeDtypeStruct((B, T, W), bf16),
        scratch_shapes=[pltpu.VMEM((ATTN_HEADS, L, hd), bf16),
                        pltpu.VMEM((ATTN_KV_HEADS, L + 2 * ATTN_BLOCK, hd), bf16),
                        pltpu.VMEM((ATTN_KV_HEADS, L + 2 * ATTN_BLOCK, hd), bf16),
                        pltpu.VMEM((ATTN_KV_HEADS, n_ctx, hd), bf16),
                        pltpu.VMEM((ATTN_KV_HEADS, n_ctx, hd), bf16)],
        compiler_params=_cparams(1), name="window_attn",
    )(sink, proj, proj, cos_t, sin_t)


def _hgrn_kernel(q_ref, i_ref, ff_ref, fb_ref, g_ref, llb_ref, lrest_ref, gain_ref, y_ref,
                 qv, kk0, kk1, cum0, cum1, yf, yb, qs0, qs1, kd0, kd1, et0, et1, st0, st1, st2, st3, *, n_ctx):
    T = q_ref.shape[0]
    K = HGRN_HEAD_DIM
    HP = 2
    TILE = ROW_TILE
    CPT = TILE // CHUNK
    n_tiles = T // TILE
    n_chunks = T // CHUNK
    n_ctx_chunks = n_ctx // CHUNK
    kk_refs, cum_refs, y_refs = (kk0, kk1), (cum0, cum1), (yf, yb)
    qs_refs, kd_refs, et_refs = (qs0, qs1), (kd0, kd1), (et0, et1)
    st_refs = (st0, st1, st2, st3)

    def chunk_rows(a):
        return a.reshape(CPT, CHUNK, a.shape[-1])

    def chunk_total(cum, d):
        c4 = chunk_rows(cum)
        return c4[:, 0:1, :] if d == 1 else c4[:, CHUNK - 1:CHUNK, :]

    rt = lax.broadcasted_iota(jnp.int32, (TILE, TILE), 0)
    ct = lax.broadcasted_iota(jnp.int32, (TILE, TILE), 1)
    same_chunk = (rt // CHUNK) == (ct // CHUNK)
    pair_ok = {0: same_chunk & (ct <= rt), 1: same_chunk & (ct >= rt)}
    pair_bf = {d: pair_ok[d].astype(bf16) for d in (0, 1)}

    llb = llb_ref[...]
    lrest = lrest_ref[...]
    one_minus_lb = jnp.exp(lrest)

    def gates(j, min_tot):
        rows = pl.ds(pl.multiple_of(j * TILE, TILE), TILE)
        qv[rows, :] = _silu(q_ref[rows, :])
        for d, z_ref in enumerate((ff_ref, fb_ref)):
            z = z_ref[rows, :]
            lf = _logaddexp(llb, lrest + _log_sigmoid(z))
            kk_refs[d][rows, :] = one_minus_lb * _sigmoid(-z)
            cum = _split_dot(pair_bf[d], lf)
            cum_refs[d][rows, :] = cum
            min_tot = jnp.minimum(min_tot, jnp.min(chunk_total(cum, d), axis=0))
        return min_tot

    min_tot = lax.fori_loop(0, n_tiles, gates, jnp.zeros((1, HP * K), f32))
    fast_ok = jnp.min(min_tot) > HGRN_FAST_MIN_TOTAL
    for st in st_refs:
        st[...] = jnp.zeros(st.shape, f32)

    def chunk_of(k, d):
        if d == 0:
            return k
        return jnp.where(k < n_ctx_chunks, n_ctx_chunks - 1 - k, n_chunks + n_ctx_chunks - 1 - k)

    def intra(j, carry):
        rows = pl.ds(pl.multiple_of(j * TILE, TILE), TILE)
        for d in (0, 1):
            cum2 = cum_refs[d][rows, :]
            tot4 = chunk_total(cum2, d)
            et_refs[d][pl.ds(pl.multiple_of(j * CPT, CPT), CPT), :, :] = jnp.broadcast_to(
                jnp.exp(tot4), (CPT, 8, HP * K))
            r2 = jnp.broadcast_to(0.5 * tot4, (CPT, CHUNK, HP * K)).reshape(TILE, HP * K)
            e1 = jnp.exp(cum2 - r2)
            e2 = jnp.exp(r2 - cum2)
            er = jnp.exp(r2)
            qa = qv[rows, :] * e1
            kb = kk_refs[d][rows, :] * e2
            qs_refs[d][rows, :] = (qa * er).astype(bf16)
            kd_refs[d][rows, :] = (kb * er).astype(bf16)
            qa = qa.astype(bf16)
            kb = kb.astype(bf16)
            for h in range(HP):
                lanes = slice(K * h, K * (h + 1))
                a = jnp.where(pair_ok[d], _dot_nt(qa[:, lanes], kb[:, lanes]), 0.0)
                y_refs[d][rows, lanes] = _dot(a.astype(bf16), i_ref[rows, lanes].astype(bf16))
        return carry

    def carry_state(k, carry):
        for d in (0, 1):
            c = chunk_of(k, d)
            rows = pl.ds(pl.multiple_of(c * CHUNK, CHUNK), CHUNK)
            et = et_refs[d][c]
            for h in range(HP):
                lanes = slice(K * h, K * (h + 1))
                st = st_refs[2 * h + d]
                s_old = st[...]
                y_refs[d][rows, lanes] += _dot_nt(qs_refs[d][rows, lanes], s_old.astype(bf16))
                st[...] = s_old * et[0:1, lanes] + _dot_tn(i_ref[rows, lanes].astype(bf16), kd_refs[d][rows, lanes])
        return carry

    @pl.when(fast_ok)
    def _():
        lax.fori_loop(0, n_tiles, intra, 0)
        lax.fori_loop(0, n_chunks, carry_state, 0, unroll=HGRN_UNROLL)

    tril = {d: _tri(CHUNK, d == 1) for d in (0, 1)}
    rows_k = lax.broadcasted_iota(jnp.int32, (CHUNK, K), 0)
    cols_c = lax.broadcasted_iota(jnp.int32, (CHUNK, CHUNK), 1)

    def exact_scores(q, kk, cum):
        def col(s, acc):
            sel = rows_k == s
            cs = jnp.sum(jnp.where(sel, cum, 0.0), axis=0, keepdims=True)
            ks = jnp.sum(jnp.where(sel, kk, 0.0), axis=0, keepdims=True)
            w = jnp.exp(jnp.minimum(cum - cs, 0.0))
            return jnp.where(cols_c == s, jnp.sum(q * ks * w, axis=1, keepdims=True), acc)
        return lax.fori_loop(0, CHUNK, col, jnp.zeros((CHUNK, CHUNK), f32))

    def exact_step(k, carry):
        for d in (0, 1):
            c = chunk_of(k, d)
            rows = pl.ds(pl.multiple_of(c * CHUNK, CHUNK), CHUNK)
            for h in range(HP):
                lanes = slice(K * h, K * (h + 1))
                cum = cum_refs[d][rows, lanes]
                kk = kk_refs[d][rows, lanes]
                q = qv[rows, lanes]
                v = i_ref[rows, lanes].astype(bf16)
                tot = cum[0:1, :] if d == 1 else cum[CHUNK - 1:CHUNK, :]
                a = jnp.where(tril[d], exact_scores(q, kk, cum), 0.0)
                st = st_refs[2 * h + d]
                s_old = st[...]
                y_refs[d][rows, lanes] = _dot(a.astype(bf16), v) + _dot_nt(
                    (q * jnp.exp(cum)).astype(bf16), s_old.astype(bf16))
                st[...] = s_old * jnp.exp(tot) + _dot_tn(v, (kk * jnp.exp(tot - cum)).astype(bf16))
        return carry

    @pl.when(jnp.logical_not(fast_ok))
    def _():
        lax.fori_loop(0, n_chunks, exact_step, 0)

    gain = gain_ref[...]

    def readout(c, carry):
        rows = pl.ds(pl.multiple_of(c * ROW_TILE, ROW_TILE), ROW_TILE)
        o = yf[rows, :] + yb[rows, :]
        gate = _silu(g_ref[rows, :])
        y_ref[rows, :] = (jnp.concatenate(
            [_rms(o[:, K * h:K * (h + 1)], gain[:, K * h:K * (h + 1)]) for h in range(HP)], axis=1) * gate
        ).astype(y_ref.dtype)
        return carry

    lax.fori_loop(0, T // ROW_TILE, readout, 0)


def _hgrn(proj, log_lb, log_rest, gain, n_ctx):
    B, T, _ = proj.shape
    wblk = 2 * HGRN_HEAD_DIM
    col = lambda base: pl.BlockSpec((None, T, wblk), lambda b, p: (b, 0, base + p))
    par = pl.BlockSpec((1, wblk), lambda b, p: (0, p))
    K = HGRN_HEAD_DIM
    return pl.pallas_call(
        functools.partial(_hgrn_kernel, n_ctx=n_ctx),
        grid=(B, HGRN_HEADS // 2),
        in_specs=[col(b0) for b0 in HGRN_BLK] + [par, par, par],
        out_specs=pl.BlockSpec((None, T, wblk), lambda b, p: (b, 0, p)),
        out_shape=jax.ShapeDtypeStruct((B, T, BRANCH_WIDTH), bf16),
        scratch_shapes=([pltpu.VMEM((T, wblk), f32)] * 7 + [pltpu.VMEM((T, wblk), bf16)] * 4
                        + [pltpu.VMEM((T // CHUNK, 8, wblk), f32)] * 2 + [pltpu.VMEM((K, K), f32)] * 4),
        compiler_params=_cparams(2), name="hgrn2",
    )(proj, proj, proj, proj, proj, log_lb, log_rest, gain)


def _ssd_kernel(z_ref, x_ref, b_ref, c_ref, dt_ref, cwx_ref, cbx_ref, cwb_ref, cbb_ref, cwc_ref, cbc_ref,
                dtbias_ref, arow_ref, skip_ref, gain_ref, y_ref,
                xstage, bstage, cstage, xs, bs, cs, xd0, xd1, cum0, cum1, dec0, dec1, yf, yb, st0, st1, *, n_ctx):
    T = x_ref.shape[0]
    P = SSD_HEAD_DIM
    R = SSD_HPG
    XW = R * P
    TILE = SCAN_CHUNK
    n_chunks = T // CHUNK
    n_ctx_chunks = n_ctx // CHUNK
    y_refs = (yf, yb)
    st_refs = (st0, st1)
    xd_refs, cum_refs, dec_refs = (xd0, xd1), (cum0, cum1), (dec0, dec1)

    rt = lax.broadcasted_iota(jnp.int32, (TILE, TILE), 0)
    ct = lax.broadcasted_iota(jnp.int32, (TILE, TILE), 1)
    same_chunk = (rt // CHUNK) == (ct // CHUNK)
    pair_bf = {0: (same_chunk & (ct <= rt)).astype(bf16), 1: (same_chunk & (ct >= rt)).astype(bf16)}
    r4 = lax.broadcasted_iota(jnp.int32, (TILE, XW), 0) % CHUNK
    c4 = lax.broadcasted_iota(jnp.int32, (TILE, XW), 1) % CHUNK
    incl4 = {0: c4 <= r4, 1: c4 >= r4}
    strict4 = {0: r4 > c4, 1: r4 < c4}

    _stage_segments(xstage, x_ref, 0, XW, n_ctx, T)
    _stage_segments(bstage, b_ref, 0, SSD_STATE, n_ctx, T)
    _stage_segments(cstage, c_ref, 0, SSD_STATE, n_ctx, T)
    cwx, cbx = cwx_ref[...], cbx_ref[...]
    cwb, cbb = cwb_ref[...], cbb_ref[...]
    cwc, cbc = cwc_ref[...], cbc_ref[...]
    dtbias, arow = dtbias_ref[...], arow_ref[...]
    src = lax.broadcasted_iota(jnp.int32, (128, 2 * XW), 0)
    dst = lax.broadcasted_iota(jnp.int32, (128, 2 * XW), 1)
    spread = (src == R * (dst // XW) + (dst % XW) // P).astype(bf16)

    def prologue(j, carry):
        off = _staged_window(j)
        rows = pl.ds(pl.multiple_of(j * SCAN_CHUNK, SCAN_CHUNK), SCAN_CHUNK)
        x = _silu(_conv_chunk(xstage, off, cwx, cbx))
        xs[rows, :] = x
        bs[rows, :] = _silu(_conv_chunk(bstage, off, cwb, cbb)).astype(bf16)
        cs[rows, :] = _silu(_conv_chunk(cstage, off, cwc, cbc)).astype(bf16)
        dt_all = _split_dot_right(_softplus(dt_ref[rows, :] + dtbias), spread)
        for d in (0, 1):
            dt = dt_all[:, XW * d:XW * (d + 1)]
            da = dt * arow[:, XW * d:XW * (d + 1)]
            xd_refs[d][rows, :] = x * dt
            cum_refs[d][rows, :] = _split_dot(pair_bf[d], da)
            logdec = _split_dot(pair_bf[d], jnp.where(strict4[d], da, 0.0))
            dec_refs[d][rows, :] = jnp.where(incl4[d], jnp.exp(logdec), 0.0)
        return carry

    lax.fori_loop(0, T // SCAN_CHUNK, prologue, 0)
    for st in st_refs:
        st[...] = jnp.zeros(st.shape, f32)

    def chunk_of(k, d):
        if d == 0:
            return k
        return jnp.where(k < n_ctx_chunks, n_ctx_chunks - 1 - k, n_chunks + n_ctx_chunks - 1 - k)

    same_head = (lax.broadcasted_iota(jnp.int32, (XW, XW), 0) // P
                 == lax.broadcasted_iota(jnp.int32, (XW, XW), 1) // P)

    def step(k, d):
        c = chunk_of(k, d)
        rows = pl.ds(pl.multiple_of(c * CHUNK, CHUNK), CHUNK)
        cm = cs[rows, :]
        bm = bs[rows, :]
        xdt = xd_refs[d][rows, :]
        cum = cum_refs[d][rows, :]
        tot = cum[0:1, :] if d == 1 else cum[CHUNK - 1:CHUNK, :]
        cb = _dot_nt(cm, jnp.concatenate([bm] * R, axis=0))
        x_heads = jnp.where(same_head, jnp.concatenate([xdt] * R, axis=0), 0.0)
        intra = _dot((cb * dec_refs[d][rows, :]).astype(bf16), x_heads.astype(bf16))
        st = st_refs[d]
        s_old = st[...]
        inter = _dot(cm, s_old.astype(bf16)) * jnp.exp(cum)
        y_refs[d][rows, :] = intra + inter
        st[...] = s_old * jnp.exp(tot) + _dot_tn(bm, (xdt * jnp.exp(tot - cum)).astype(bf16))

    def body(k, carry):
        step(k, 0)
        step(k, 1)
        return carry

    lax.fori_loop(0, n_chunks, body, 0, unroll=SSD_UNROLL)

    skip, gain = skip_ref[...], gain_ref[...]

    def readout(c, carry):
        rows = pl.ds(pl.multiple_of(c * ROW_TILE, ROW_TILE), ROW_TILE)
        y = skip * xs[rows, :] + yf[rows, :] + yb[rows, :]
        y_ref[rows, :] = _rms(y * _silu(z_ref[rows, :]), gain).astype(y_ref.dtype)
        return carry

    lax.fori_loop(0, T // ROW_TILE, readout, 0)


def _ssd(proj, cwx, cbx, cwb, cbb, cwc, cbc, dtbias, arow, skip, gain, n_ctx):
    B, T, _ = proj.shape
    XW = SSD_HPG * SSD_HEAD_DIM
    N = SSD_STATE
    col = lambda base, w: pl.BlockSpec((None, T, w), lambda b, g: (b, 0, base + g))
    par = lambda r, w: pl.BlockSpec((None, r, w), lambda b, g: (g, 0, 0))
    return pl.pallas_call(
        functools.partial(_ssd_kernel, n_ctx=n_ctx),
        grid=(B, SSD_GROUPS),
        in_specs=[col(SSD_Z_BLK, XW), col(SSD_X_BLK, XW), col(SSD_B_BLK, N), col(SSD_C_BLK, N), col(SSD_DT_BLK, 128),
                  par(CONV_WIDTH, XW), par(1, XW), par(CONV_WIDTH, N), par(1, N), par(CONV_WIDTH, N), par(1, N),
                  par(1, 128), par(1, 2 * XW), par(1, XW), par(1, XW)],
        out_specs=pl.BlockSpec((None, T, XW), lambda b, g: (b, 0, g)),
        out_shape=jax.ShapeDtypeStruct((B, T, SSD_WIDTH), bf16),
        scratch_shapes=[pltpu.VMEM((T + 3 * SEG_PAD, XW), f32), pltpu.VMEM((T + 3 * SEG_PAD, N), f32),
                        pltpu.VMEM((T + 3 * SEG_PAD, N), f32),
                        pltpu.VMEM((T, XW), f32), pltpu.VMEM((T, N), bf16), pltpu.VMEM((T, N), bf16),
                        pltpu.VMEM((T, XW), f32), pltpu.VMEM((T, XW), f32), pltpu.VMEM((T, XW), f32),
                        pltpu.VMEM((T, XW), f32), pltpu.VMEM((T, XW), f32), pltpu.VMEM((T, XW), f32),
                        pltpu.VMEM((T, XW), f32), pltpu.VMEM((T, XW), f32),
                        pltpu.VMEM((N, XW), f32), pltpu.VMEM((N, XW), f32)],
        compiler_params=_cparams(2), name="ssd",
    )(proj, proj, proj, proj, proj, cwx, cbx, cwb, cbb, cwc, cbc, dtbias, arow, skip, gain)


def _post_kernel(x_ref, gl_ref, ya_ref, yb_ref, yc_ref, yd_ref, m_ref, wbr_ref, wout_ref, wup_ref, wdn_ref,
                 g1_ref, g2_ref, g3_ref, *rest, emit_next):
    D = D_MODEL
    ys = (ya_ref, yb_ref, yc_ref, yd_ref)
    merged = None
    for i in range(N_BRANCH):
        t = _sigmoid(gl_ref[:, D * i:D * (i + 1)]) * _dot(ys[i][...], wbr_ref[i])
        merged = t if merged is None else merged + t
    mix = _dot(merged.astype(bf16), wout_ref[...])
    mod = lambda k: m_ref[:, D * k:D * (k + 1)]
    x1 = x_ref[...] + mod(2) * _rms(mix, g1_ref[...])
    h2 = (_rms(x1, g2_ref[...]) * (1.0 + mod(4)) + mod(3)).astype(bf16)
    down = None
    hc = 1024
    for c in range(MLP_HIDDEN // hc):
        u = jnp.maximum(_dot(h2, wup_ref[:, hc * c:hc * (c + 1)]), 0.0)
        t = _dot((u * u).astype(bf16), wdn_ref[hc * c:hc * (c + 1), :])
        down = t if down is None else down + t
    x2 = x1 + mod(5) * _rms(down, g3_ref[...])
    if emit_next:
        mn_ref, gn_ref, o_ref, h_ref = rest
        h_ref[...] = _pre_norm(x2, mn_ref, gn_ref)
    else:
        (o_ref,) = rest
    o_ref[...] = x2


def _post(xs, proj, ya, yb, yc, yd, mods_l, wbr, wout, wup, wdn, g1, g2, g3, nxt):
    B, T, D = xs.shape
    W = BRANCH_WIDTH
    nt = T // ROW_TILE
    first = 1 if nxt is None else 0
    tile = lambda w: pl.BlockSpec((None, ROW_TILE, w), lambda b, i: (b, i + first, 0))
    once = lambda *s: pl.BlockSpec(s, lambda b, i: (0,) * len(s), pipeline_mode=pl.Buffered(1))
    mod_idx = _mod_row_index(B)
    mod_row = pl.BlockSpec((None, 1, 6 * D), lambda b, i: mod_idx(b, i + first))
    out_tile = pl.BlockSpec((None, ROW_TILE, D), lambda b, i: (b, i, 0))
    in_specs = [tile(D), tile(MERGE_COLS), tile(W), tile(W), tile(W), tile(W), mod_row,
                once(N_BRANCH, W, D), once(D, D), once(D, MLP_HIDDEN), once(MLP_HIDDEN, D),
                once(1, D), once(1, D), once(1, D)]
    args = [xs, proj, ya, yb, yc, yd, mods_l, wbr, wout, wup, wdn, g1, g2, g3]
    out_specs, out_shape = out_tile, jax.ShapeDtypeStruct((B, T - first * ROW_TILE, D), f32)
    if nxt is not None:
        in_specs += [mod_row, once(1, D)]
        args += list(nxt)
        out_specs = [out_tile, out_tile]
        out_shape = [out_shape, jax.ShapeDtypeStruct((B, T, D), bf16)]
    return pl.pallas_call(
        functools.partial(_post_kernel, emit_next=nxt is not None),
        grid=(B, nt - first), in_specs=in_specs, out_specs=out_specs, out_shape=out_shape,
        compiler_params=_cparams(2), name="merge_mlp",
    )(*args)


def _block_diag(w):
    eye = jnp.eye(LRU_BLOCKS, dtype=w.dtype)
    return jnp.einsum('hij,hg->higj', w, eye).reshape(LRU_WIDTH, LRU_WIDTH)


def _rope_tables(n_lat):
    half = ATTN_HEAD_DIM // 2
    quarter = half // 2
    inv_freq = ROPE_BASE ** (-jnp.arange(quarter, dtype=f32) / quarter)
    t = jnp.arange(n_lat, dtype=jnp.int32)
    rows = (t // GRID_W).astype(f32)[:, None] * inv_freq
    cols = (t % GRID_W).astype(f32)[:, None] * inv_freq
    cos_h = jnp.concatenate([jnp.cos(rows), jnp.cos(rows), jnp.cos(cols), jnp.cos(cols)], axis=-1)
    sin_h = jnp.concatenate([-jnp.sin(rows), jnp.sin(rows), -jnp.sin(cols), jnp.sin(cols)], axis=-1)
    return jnp.tile(cos_h, (1, 2)), jnp.tile(sin_h, (1, 2))


def kernel(x, c, ctx, c_ctx, w_ada, b_ada, g_pre_mix, g_post_mix, g_pre_mlp, g_post_mlp, w_in, lru_conv_w, lru_conv_b, lru_rec_w, lru_rec_b, lru_inp_w, lru_inp_b, lru_lambda, attn_sink, hgrn_lb_logits, hgrn_norm_g, ssd_conv_w, ssd_conv_b, ssd_dt_bias, ssd_a_log, ssd_skip, ssd_norm_g, w_branch, w_out, w_mlp_up, w_mlp_down):
    B, L, D = x.shape
    n_ctx = ctx.shape[1]
    assert D == D_MODEL and n_ctx == ROW_TILE == SCAN_CHUNK and L % ROW_TILE == 0 and L % GRID_W == 0
    assert w_in.shape == (DEPTH, D_MODEL, IN_COLS)

    pad_rows = (-(B + 1)) % 8
    c_all = jnp.concatenate([c, c_ctx[None, :], jnp.zeros((pad_rows, D), f32)], axis=0)
    mods = _ada(c_all, w_ada, b_ada)
    mods = mods.reshape(DEPTH, c_all.shape[0], 1, 6 * D)

    lb = jnp.cumsum(jax.nn.softmax(hgrn_lb_logits.astype(f32), axis=0), axis=0)
    lb = lb - lb[0]
    cos_t, sin_t = _rope_tables(L)
    G, R, P, N = SSD_GROUPS, SSD_HPG, SSD_HEAD_DIM, SSD_STATE

    for l in range(DEPTH):
        last = l == DEPTH - 1
        wl = w_in[l]
        dtf, dtb = wl[:, 5888:5896], wl[:, 5896:5904]
        zpad = jnp.zeros((D, 128 - 2 * R), f32)
        dt_cols = [jnp.concatenate([dtf[:, R * g:R * (g + 1)], dtb[:, R * g:R * (g + 1)], zpad], axis=1)
                   for g in range(G)]
        w_perm = jnp.concatenate([wl[:, 5904:], wl[:, :5888]] + dt_cols, axis=1).astype(bf16)

        if l == 0:
            xs, h = _first_norm(ctx, x, mods[0], g_pre_mix[0][None, :])
        proj = _in_proj(h, w_perm)

        wbd = jnp.concatenate([_block_diag(lru_rec_w[l, 0]), _block_diag(lru_inp_w[l, 0]),
                               _block_diag(lru_rec_w[l, 1]), _block_diag(lru_inp_w[l, 1])], axis=1).astype(bf16)
        gb = jnp.concatenate([lru_rec_b[l, 0], lru_inp_b[l, 0], lru_rec_b[l, 1], lru_inp_b[l, 1]])[None, :]
        csp = LRU_C * jax.nn.softplus(-lru_lambda[l])
        ya = _lru(proj, lru_conv_w[l], lru_conv_b[l][None, :], wbd, gb, csp, n_ctx)

        yb = _attn(proj, attn_sink[l], cos_t, sin_t, n_ctx)

        lbl = lb[l][None, :]
        yc = _hgrn(proj, jnp.log(lbl), jnp.log1p(-lbl), hgrn_norm_g[l][None, :], n_ctx)

        cw, cbv = ssd_conv_w[l], ssd_conv_b[l]
        XW = R * P
        grp = lambda a, off, w: jnp.stack([a[..., off + w * g:off + w * (g + 1)] for g in range(G)], axis=0)
        cwx, cbx = grp(cw, 0, XW), grp(cbv[None, :], 0, XW)
        cwb, cbb = grp(cw, SSD_WIDTH, N), grp(cbv[None, :], SSD_WIDTH, N)
        cwc, cbc = grp(cw, SSD_WIDTH + G * N, N), grp(cbv[None, :], SSD_WIDTH + G * N, N)
        a_neg = -jnp.exp(ssd_a_log[l].astype(f32))
        per_group = lambda v: [jnp.concatenate([v[0, R * g:R * (g + 1)], v[1, R * g:R * (g + 1)]]) for g in range(G)]
        lanes_compact = lambda v: jnp.stack([jnp.pad(u, (0, 128 - 2 * R))[None, :] for u in per_group(v)], axis=0)
        lanes_spread = lambda v: jnp.stack([jnp.repeat(u, P)[None, :] for u in per_group(v)], axis=0)
        skip = jnp.repeat(ssd_skip[l], P).reshape(G, 1, XW)
        gain_d = ssd_norm_g[l].reshape(G, 1, XW)
        yd = _ssd(proj, cwx, cbx, cwb, cbb, cwc, cbc, lanes_compact(ssd_dt_bias[l]), lanes_spread(a_neg), skip, gain_d, n_ctx)

        nxt = None if last else (mods[l + 1], g_pre_mix[l + 1][None, :])
        out = _post(xs, proj, ya, yb, yc, yd, mods[l], w_branch[l].astype(bf16), w_out[l].astype(bf16),
                    w_mlp_up[l].astype(bf16), w_mlp_down[l].astype(bf16),
                    g_post_mix[l][None, :], g_pre_mlp[l][None, :], g_post_mlp[l][None, :], nxt)
        if last:
            return out
        xs, h = out
```

```python
import functools

import numpy as np
import jax
import jax.numpy as jnp
from jax import lax
from jax.experimental import pallas as pl
from jax.experimental.pallas import tpu as pltpu

f32 = jnp.float32
bf16 = jnp.bfloat16

D_MODEL = 1024
DEPTH = 2
GRID_W = 64
N_BRANCH = 4
BRANCH_WIDTH = D_MODEL // 2
CONV_WIDTH = 4
LRU_WIDTH = BRANCH_WIDTH
LRU_BLOCKS = 8
LRU_BLOCK_W = LRU_WIDTH // LRU_BLOCKS
LRU_C = 8.0
ATTN_HEAD_DIM = 64
ATTN_HEADS = BRANCH_WIDTH // ATTN_HEAD_DIM
ATTN_KV_HEADS = 2
ATTN_REP = ATTN_HEADS // ATTN_KV_HEADS
ATTN_WINDOW = 128
ATTN_BLOCK = 128
ROPE_BASE = 10000.0
HGRN_HEAD_DIM = 128
HGRN_HEADS = BRANCH_WIDTH // HGRN_HEAD_DIM
SSD_WIDTH = BRANCH_WIDTH
SSD_HEAD_DIM = 64
SSD_HEADS = SSD_WIDTH // SSD_HEAD_DIM
SSD_GROUPS = 2
SSD_HPG = SSD_HEADS // SSD_GROUPS
SSD_STATE = 128
MLP_HIDDEN = 4 * D_MODEL
NORM_EPS = 1e-6

IN_COLS = 10000
MERGE_COLS = N_BRANCH * D_MODEL
PROJ_COLS = 10240
LRU_BLK = 4
ATTN_Q_BLK = 10
ATTN_KV_BLK = 22
HGRN_BLK = (23, 25, 27, 29, 31)
SSD_Z_BLK = 33
SSD_X_BLK = 35
SSD_B_BLK = 74
SSD_C_BLK = 76
SSD_DT_BLK = 78

ROW_TILE = 256
SCAN_CHUNK = 256
CHUNK = 64
SEG_PAD = 8
LRU_U_PITCH = 56
LRU_G_PITCH = 40
HGRN_FAST_MIN_TOTAL = -160.0
HGRN_UNROLL = 6
SSD_UNROLL = 6
MASK_BIAS = -1e30
VMEM_LIMIT = 58 * 1024 * 1024


def _cparams(n_axes):
    return pltpu.CompilerParams(dimension_semantics=("arbitrary",) * n_axes, vmem_limit_bytes=VMEM_LIMIT)


def _sigmoid(x):
    return 0.5 * jnp.tanh(0.5 * x) + 0.5


def _silu(x):
    return x * jax.nn.sigmoid(x)


def _gelu_tanh(x):
    return 0.5 * x * (1.0 + jnp.tanh(np.float32(np.sqrt(2.0 / np.pi)) * (x + 0.044715 * (x * x * x))))


def _expm1_given_exp(x, ex):
    poly = x * (1.0 + x * (1.0 / 2) * (1.0 + x * (1.0 / 3) * (1.0 + x * (1.0 / 4))))
    return jnp.where(jnp.abs(x) < 0.03, poly, ex - 1.0)


def _sqrt_nonneg(y):
    return jnp.where(y > 0.0, y * lax.rsqrt(y), 0.0)


def _log_sigmoid(x):
    return jnp.minimum(x, 0.0) - jnp.log(1.0 + jnp.exp(-jnp.abs(x)))


def _logaddexp(a, b):
    return jnp.maximum(a, b) + jnp.log(1.0 + jnp.exp(-jnp.abs(a - b)))


def _softplus(x):
    return jnp.maximum(x, 0.0) + jnp.log1p(jnp.exp(-jnp.abs(x)))


def _rms(x, g):
    return x * lax.rsqrt(jnp.mean(x * x, axis=-1, keepdims=True) + NORM_EPS) * g


def _dot(a, b):
    return jnp.dot(a, b, preferred_element_type=f32)


def _dot_nt(a, b):
    return lax.dot_general(a, b, (((1,), (1,)), ((), ())), preferred_element_type=f32)


def _dot_tn(a, b):
    return lax.dot_general(a, b, (((0,), (0,)), ((), ())), preferred_element_type=f32)


def _split_dot(tri, x):
    hi = x.astype(bf16)
    lo = (x - hi.astype(f32)).astype(bf16)
    return _dot(tri, hi) + _dot(tri, lo)


def _split_dot_right(x, sel):
    hi = x.astype(bf16)
    lo = (x - hi.astype(f32)).astype(bf16)
    return _dot(hi, sel) + _dot(lo, sel)


def _tri(n, reverse):
    r = lax.broadcasted_iota(jnp.int32, (n, n), 0)
    c = lax.broadcasted_iota(jnp.int32, (n, n), 1)
    return (c >= r) if reverse else (c <= r)


def _ada_kernel(c_ref, w_ref, b_ref, o_ref):
    cond = _silu(c_ref[...])
    o_ref[...] = jnp.dot(cond, w_ref[...], preferred_element_type=f32,
                         precision=lax.Precision.HIGHEST) + b_ref[...]


def _ada(c_all, w_ada, b_ada):
    rows = c_all.shape[0]
    nt = 6 * D_MODEL // 1024
    return pl.pallas_call(
        _ada_kernel,
        grid=(DEPTH, nt),
        in_specs=[pl.BlockSpec((rows, D_MODEL), lambda l, j: (0, 0)),
                  pl.BlockSpec((None, D_MODEL, 1024), lambda l, j: (l, 0, j)),
                  pl.BlockSpec((None, 1, 1024), lambda l, j: (l, 0, j))],
        out_specs=pl.BlockSpec((None, rows, 1024), lambda l, j: (l, 0, j)),
        out_shape=jax.ShapeDtypeStruct((DEPTH, rows, 6 * D_MODEL), f32),
        compiler_params=_cparams(2), name="ada_mod",
    )(c_all, w_ada, b_ada.reshape(DEPTH, 1, 6 * D_MODEL))


def _mod_row_index(n_batch):
    return lambda b, i: (jnp.where(i == 0, n_batch, b), 0, 0)


def _pre_norm(x, m_ref, g_ref):
    shift = m_ref[:, 0:D_MODEL]
    scale = m_ref[:, D_MODEL:2 * D_MODEL]
    return (_rms(x, g_ref[...]) * (1.0 + scale) + shift).astype(bf16)


def _first_norm_kernel(ctx_ref, x_ref, m_ref, g_ref, xs_ref, h_ref):
    @pl.when(pl.program_id(1) == 0)
    def _():
        xs_ref[...] = ctx_ref[...]

    @pl.when(pl.program_id(1) > 0)
    def _():
        xs_ref[...] = x_ref[...]

    h_ref[...] = _pre_norm(xs_ref[...], m_ref, g_ref)


def _first_norm(ctx, x, mods_l, g):
    B, L, D = x.shape
    T = ctx.shape[1] + L
    tile = pl.BlockSpec((None, ROW_TILE, D), lambda b, i: (b, i, 0))
    return pl.pallas_call(
        _first_norm_kernel,
        grid=(B, T // ROW_TILE),
        in_specs=[pl.BlockSpec((None, ROW_TILE, D), lambda b, i: (b, 0, 0)),
                  pl.BlockSpec((None, ROW_TILE, D), lambda b, i: (b, jnp.maximum(i - 1, 0), 0)),
                  pl.BlockSpec((None, 1, 6 * D), _mod_row_index(B)),
                  pl.BlockSpec((1, D), lambda b, i: (0, 0))],
        out_specs=[tile, tile],
        out_shape=[jax.ShapeDtypeStruct((B, T, D), f32), jax.ShapeDtypeStruct((B, T, D), bf16)],
        compiler_params=_cparams(2), name="first_norm",
    )(ctx, x, mods_l, g)


def _proj_kernel(a_ref, w_ref, o_ref):
    o_ref[...] = _dot(a_ref[...], w_ref[...])


def _in_proj(h, w_perm):
    B, T, _ = h.shape
    tn = 1024
    return pl.pallas_call(
        _proj_kernel,
        grid=(B, PROJ_COLS // tn),
        in_specs=[pl.BlockSpec((None, T, D_MODEL), lambda b, j: (b, 0, 0)),
                  pl.BlockSpec((D_MODEL, tn), lambda b, j: (0, j))],
        out_specs=pl.BlockSpec((None, T, tn), lambda b, j: (b, 0, j)),
        out_shape=jax.ShapeDtypeStruct((B, T, PROJ_COLS), f32),
        compiler_params=_cparams(2), name="in_proj",
    )(h, w_perm)


def _stage_segments(dst, src_ref, col0, width, n_ctx, t_all):
    z = jnp.zeros((SEG_PAD, width), f32)
    dst[0:SEG_PAD, :] = z
    dst[SEG_PAD:SEG_PAD + n_ctx, :] = src_ref[0:n_ctx, col0:col0 + width]
    dst[SEG_PAD + n_ctx:2 * SEG_PAD + n_ctx, :] = z
    dst[2 * SEG_PAD + n_ctx:2 * SEG_PAD + t_all, :] = src_ref[n_ctx:t_all, col0:col0 + width]
    dst[2 * SEG_PAD + t_all:3 * SEG_PAD + t_all, :] = z


def _staged_window(j):
    return pl.multiple_of(SCAN_CHUNK * j + jnp.where(j >= 1, SEG_PAD, 0), SEG_PAD)


def _conv_chunk(stage_ref, win0, cw, cb):
    n = SCAN_CHUNK + 2 * SEG_PAD
    win = stage_ref[pl.ds(win0, n), :]
    acc = cb + cw[2:3, :] * win[SEG_PAD:SEG_PAD + SCAN_CHUNK]
    for tap, o in ((0, -2), (1, -1), (3, 1)):
        acc = acc + cw[tap:tap + 1, :] * pltpu.roll(win, (-o) % n, 0)[SEG_PAD:SEG_PAD + SCAN_CHUNK]
    return acc


def _lru_kernel(p_ref, cw_ref, cb_ref, wbd_ref, gb_ref, csp_ref, y_ref, stage, useg, gseg, hfwd, *, n_ctx):
    T = p_ref.shape[0]
    W = LRU_WIDTH
    SEG = SCAN_CHUNK // 8
    n_tiles = W // 128
    n_chunks = T // SCAN_CHUNK
    for c in range(n_tiles):
        _stage_segments(stage.at[c], p_ref, 128 * c, 128, n_ctx, T)
        for j in range(n_chunks):
            win0 = SCAN_CHUNK * j + (SEG_PAD if j >= 1 else 0)
            for s in range(8):
                slot = 8 * j + s
                useg[c, LRU_U_PITCH * slot:LRU_U_PITCH * slot + SEG + 2 * SEG_PAD, :] = stage[
                    c, win0 + SEG * s:win0 + SEG * (s + 1) + 2 * SEG_PAD, :]
                gseg[c, LRU_G_PITCH * slot:LRU_G_PITCH * slot + SEG, :] = p_ref[
                    SCAN_CHUNK * j + SEG * s:SCAN_CHUNK * j + SEG * (s + 1), W + 128 * c:W + 128 * (c + 1)]
    cw = cw_ref[...]
    cb = cb_ref[...]

    def segment_rows(ref, row0, pitch):
        return jnp.concatenate([ref[c, pl.ds(row0, 8, stride=pitch), :] for c in range(n_tiles)], axis=1)

    def chunk(j, carry, d):
        reverse = d == 1
        u0 = pl.multiple_of(j * (8 * LRU_U_PITCH), 8) + SEG_PAD
        v = {k: segment_rows(useg, u0 + k, LRU_U_PITCH) for k in range(-2, SEG + 1)}
        x = jnp.concatenate(
            [cb + cw[0:1, :] * v[k - 2] + cw[1:2, :] * v[k - 1] + cw[2:3, :] * v[k] + cw[3:4, :] * v[k + 1]
             for k in range(SEG)], axis=0)
        g = _dot(x.astype(bf16), wbd_ref[:, 2 * W * d:2 * W * (d + 1)]) + gb_ref[:, 2 * W * d:2 * W * (d + 1)]
        r = _sigmoid(g[:, 0:W])
        i = _sigmoid(g[:, W:2 * W])
        log_a = -csp_ref[d:d + 1, :] * r
        a = jnp.exp(log_a)
        b = _sqrt_nonneg(-_expm1_given_exp(2.0 * log_a, a * a)) * (i * x)
        order = range(SEG - 1, -1, -1) if reverse else range(SEG)
        hl, pk = [None] * SEG, [None] * SEG
        h = p = None
        for k in order:
            ak, bk = a[8 * k:8 * k + 8], b[8 * k:8 * k + 8]
            h = bk if h is None else ak * h + bk
            p = ak if p is None else ak * p
            hl[k], pk[k] = h, p
        seg_in = [None] * 8
        for s in (range(7, -1, -1) if reverse else range(8)):
            seg_in[s] = carry
            carry = h[s:s + 1, :] + p[s:s + 1, :] * carry
        seg_in = jnp.concatenate(seg_in, axis=0)
        return [hl[k] + pk[k] * seg_in for k in range(SEG)], carry

    zero = jnp.zeros((1, W), f32)

    def fwd_body(j, carry):
        hs, carry = chunk(j, carry, 0)
        base = pl.multiple_of(j * SCAN_CHUNK, SCAN_CHUNK)
        for k in range(SEG):
            hfwd[pl.ds(base + 8 * k, 8), :] = hs[k]
        return carry

    lax.fori_loop(0, n_chunks, fwd_body, zero)

    def bwd_body(n, carry):
        j = jnp.where(n == 0, 0, n_chunks - n)
        hs, carry = chunk(j, carry, 1)
        base = pl.multiple_of(j * SCAN_CHUNK, SCAN_CHUNK)
        g0 = pl.multiple_of(j * (8 * LRU_G_PITCH), 8)
        for k in range(SEG):
            yk = (hfwd[pl.ds(base + 8 * k, 8), :] + hs[k]) * _gelu_tanh(segment_rows(gseg, g0 + k, LRU_G_PITCH))
            for c in range(n_tiles):
                gseg[c, pl.ds(g0 + k, 8, stride=LRU_G_PITCH), :] = yk[:, 128 * c:128 * (c + 1)]
        for c in range(n_tiles):
            for s in range(8):
                y_ref[pl.ds(base + SEG * s, SEG), 128 * c:128 * (c + 1)] = gseg[
                    c, pl.ds(g0 + LRU_G_PITCH * s, SEG), :].astype(y_ref.dtype)
        return carry

    lax.fori_loop(0, n_chunks, bwd_body, zero)


def _lru(proj, cw, cb, wbd, gb, csp, n_ctx):
    B, T, _ = proj.shape
    W = LRU_WIDTH
    full = lambda *s: pl.BlockSpec(s, lambda b: (0,) * len(s))
    return pl.pallas_call(
        functools.partial(_lru_kernel, n_ctx=n_ctx),
        grid=(B,),
        in_specs=[pl.BlockSpec((None, T, 2 * W), lambda b: (b, 0, LRU_BLK)),
                  full(CONV_WIDTH, W), full(1, W), full(W, 4 * W), full(1, 4 * W), full(2, W)],
        out_specs=pl.BlockSpec((None, T, W), lambda b: (b, 0, 0)),
        out_shape=jax.ShapeDtypeStruct((B, T, W), bf16),
        scratch_shapes=[pltpu.VMEM((W // 128, T + 3 * SEG_PAD, 128), f32),
                        pltpu.VMEM((W // 128, T // SCAN_CHUNK * 8 * LRU_U_PITCH, 128), f32),
                        pltpu.VMEM((W // 128, T // SCAN_CHUNK * 8 * LRU_G_PITCH, 128), f32),
                        pltpu.VMEM((T, W), f32)],
        compiler_params=_cparams(1), name="rglru",
    )(proj, cw, cb, wbd, gb, csp)


def _attn_kernel(sink_ref, q_ref, kv_ref, cos_ref, sin_ref, y_ref, qs, kp, vp, kc, vc, *, n_ctx):
    T = q_ref.shape[0]
    L = T - n_ctx
    hd = ATTN_HEAD_DIM
    G, R = ATTN_KV_HEADS, ATTN_REP
    nb = L // ATTN_BLOCK
    span = 3 * ATTN_BLOCK
    scale = np.float32(hd ** -0.5)

    def swap_halves(x):
        lane = lax.broadcasted_iota(jnp.int32, x.shape, 1)
        return jnp.where((lane & 31) < 16, pltpu.roll(x, 128 - 16, 1), pltpu.roll(x, 16, 1))

    zpad = jnp.zeros((ATTN_BLOCK, hd), bf16)
    for g in range(G):
        kp[g, 0:ATTN_BLOCK, :] = zpad
        kp[g, ATTN_BLOCK + L:2 * ATTN_BLOCK + L, :] = zpad
        vp[g, 0:ATTN_BLOCK, :] = zpad
        vp[g, ATTN_BLOCK + L:2 * ATTN_BLOCK + L, :] = zpad
        kc[g] = kv_ref[0:n_ctx, hd * g:hd * g + hd].astype(bf16)
        vc[g] = kv_ref[0:n_ctx, hd * (G + g):hd * (G + g) + hd].astype(bf16)

    def rope_rows(c, carry):
        r0 = pl.multiple_of(c * ATTN_BLOCK, ATTN_BLOCK)
        cos = cos_ref[pl.ds(r0, ATTN_BLOCK), :]
        sin = sin_ref[pl.ds(r0, ATTN_BLOCK), :]
        for s in range(ATTN_HEADS * hd // 128):
            xq = q_ref[pl.ds(n_ctx + r0, ATTN_BLOCK), 128 * s:128 * s + 128]
            xr = ((xq * cos + swap_halves(xq) * sin) * scale).astype(bf16)
            qs[2 * s, pl.ds(r0, ATTN_BLOCK), :] = xr[:, 0:hd]
            qs[2 * s + 1, pl.ds(r0, ATTN_BLOCK), :] = xr[:, hd:2 * hd]
        xk = kv_ref[pl.ds(n_ctx + r0, ATTN_BLOCK), 0:128]
        xr = (xk * cos + swap_halves(xk) * sin).astype(bf16)
        xv = kv_ref[pl.ds(n_ctx + r0, ATTN_BLOCK), 128:256].astype(bf16)
        for g in range(G):
            kp[g, pl.ds(ATTN_BLOCK + r0, ATTN_BLOCK), :] = xr[:, hd * g:hd * g + hd]
            vp[g, pl.ds(ATTN_BLOCK + r0, ATTN_BLOCK), :] = xv[:, hd * g:hd * g + hd]
        return carry

    lax.fori_loop(0, nb, rope_rows, 0)

    def sink_column(g, rows_per_head):
        rows = lax.broadcasted_iota(jnp.int32, (R * rows_per_head, 1), 0)
        col = jnp.full((R * rows_per_head, 1), sink_ref[R * g + R - 1], f32)
        for r in range(R - 2, -1, -1):
            col = jnp.where(rows < (r + 1) * rows_per_head, sink_ref[R * g + r], col)
        return col

    def softmax_pv(pieces, sink, v):
        blocks = [s[:, 128 * j:128 * (j + 1)] for s in pieces for j in range(s.shape[1] // 128)]
        m = functools.reduce(jnp.maximum, blocks)
        m = jnp.maximum(jnp.max(m, axis=-1, keepdims=True), sink)
        ps = [jnp.exp(s - m) for s in blocks]
        den = jnp.sum(functools.reduce(jnp.add, ps), axis=-1, keepdims=True) + jnp.exp(sink - m)
        return _dot(jnp.concatenate([p.astype(bf16) for p in ps], axis=1), v) * (1.0 / den)

    qi = lax.broadcasted_iota(jnp.int32, (R * ATTN_BLOCK, ATTN_BLOCK), 0) & (ATTN_BLOCK - 1)
    kj = lax.broadcasted_iota(jnp.int32, (R * ATTN_BLOCK, ATTN_BLOCK), 1)
    bias_a = jnp.where(kj - qi >= ATTN_BLOCK - ATTN_WINDOW, 0.0, MASK_BIAS).astype(f32)
    bias_c = jnp.where(kj - qi <= ATTN_WINDOW - ATTN_BLOCK, 0.0, MASK_BIAS).astype(f32)

    def block(n, carry):
        r0 = pl.multiple_of(n * ATTN_BLOCK, ATTN_BLOCK)
        ba = bias_a + jnp.where(n == 0, MASK_BIAS, 0.0)
        bc = bias_c + jnp.where(n == nb - 1, MASK_BIAS, 0.0)
        for g in range(G):
            q = jnp.concatenate([qs[R * g + r, pl.ds(r0, ATTN_BLOCK), :] for r in range(R)], axis=0)
            k = jnp.concatenate([kp[g, pl.ds(r0, span), :], kc[g]], axis=0)
            v = jnp.concatenate([vp[g, pl.ds(r0, span), :], vc[g]], axis=0)
            s = _dot_nt(q, k)
            pieces = [s[:, 0:ATTN_BLOCK] + ba, s[:, ATTN_BLOCK:2 * ATTN_BLOCK],
                      s[:, 2 * ATTN_BLOCK:span] + bc, s[:, span:]]
            o = softmax_pv(pieces, sink_column(g, ATTN_BLOCK), v)
            y_ref[pl.ds(n_ctx + r0, ATTN_BLOCK), R * hd * g:R * hd * (g + 1)] = jnp.concatenate(
                [o[ATTN_BLOCK * r:ATTN_BLOCK * (r + 1)] for r in range(R)], axis=1).astype(y_ref.dtype)
        return carry

    lax.fori_loop(0, nb, block, 0)

    for g in range(G):
        q = jnp.concatenate(
            [(q_ref[0:n_ctx, hd * (R * g + r):hd * (R * g + r + 1)] * scale).astype(bf16) for r in range(R)], axis=0)
        o = softmax_pv([_dot_nt(q, kc[g])], sink_column(g, n_ctx), vc[g])
        y_ref[0:n_ctx, R * hd * g:R * hd * (g + 1)] = jnp.concatenate(
            [o[n_ctx * r:n_ctx * (r + 1)] for r in range(R)], axis=1).astype(y_ref.dtype)


def _attn(proj, sink, cos_t, sin_t, n_ctx):
    B, T, _ = proj.shape
    L = T - n_ctx
    W = BRANCH_WIDTH
    hd = ATTN_HEAD_DIM
    return pl.pallas_call(
        functools.partial(_attn_kernel, n_ctx=n_ctx),
        grid=(B,),
        in_specs=[pl.BlockSpec(memory_space=pltpu.SMEM),
                  pl.BlockSpec((None, T, W), lambda b: (b, 0, ATTN_Q_BLK)),
                  pl.BlockSpec((None, T, 256), lambda b: (b, 0, ATTN_KV_BLK)),
                  pl.BlockSpec((L, 128), lambda b: (0, 0)),
                  pl.BlockSpec((L, 128), lambda b: (0, 0))],
        out_specs=pl.BlockSpec((None, T, W), lambda b: (b, 0, 0)),
        out_shape=jax.ShapeDtypeStruct((B, T, W), bf16),
        scratch_shapes=[pltpu.VMEM((ATTN_HEADS, L, hd), bf16),
                        pltpu.VMEM((ATTN_KV_HEADS, L + 2 * ATTN_BLOCK, hd), bf16),
                        pltpu.VMEM((ATTN_KV_HEADS, L + 2 * ATTN_BLOCK, hd), bf16),
                        pltpu.VMEM((ATTN_KV_HEADS, n_ctx, hd), bf16),
                        pltpu.VMEM((ATTN_KV_HEADS, n_ctx, hd), bf16)],
        compiler_params=_cparams(1), name="window_attn",
    )(sink, proj, proj, cos_t, sin_t)


def _hgrn_kernel(q_ref, i_ref, ff_ref, fb_ref, g_ref, llb_ref, lrest_ref, gain_ref, y_ref,
                 qv, kk0, kk1, cum0, cum1, yf, yb, qs0, qs1, kd0, kd1, et0, et1, st0, st1, st2, st3, *, n_ctx):
    T = q_ref.shape[0]
    K = HGRN_HEAD_DIM
    HP = 2
    TILE = ROW_TILE
    CPT = TILE // CHUNK
    n_tiles = T // TILE
    n_chunks = T // CHUNK
    n_ctx_chunks = n_ctx // CHUNK
    kk_refs, cum_refs, y_refs = (kk0, kk1), (cum0, cum1), (yf, yb)
    qs_refs, kd_refs, et_refs = (qs0, qs1), (kd0, kd1), (et0, et1)
    st_refs = (st0, st1, st2, st3)

    def chunk_rows(a):
        return a.reshape(CPT, CHUNK, a.shape[-1])

    def chunk_total(cum, d):
        c4 = chunk_rows(cum)
        return c4[:, 0:1, :] if d == 1 else c4[:, CHUNK - 1:CHUNK, :]

    rt = lax.broadcasted_iota(jnp.int32, (TILE, TILE), 0)
    ct = lax.broadcasted_iota(jnp.int32, (TILE, TILE), 1)
    same_chunk = (rt // CHUNK) == (ct // CHUNK)
    pair_ok = {0: same_chunk & (ct <= rt), 1: same_chunk & (ct >= rt)}
    pair_bf = {d: pair_ok[d].astype(bf16) for d in (0, 1)}

    llb = llb_ref[...]
    lrest = lrest_ref[...]
    one_minus_lb = jnp.exp(lrest)

    def gates(j, min_tot):
        rows = pl.ds(pl.multiple_of(j * TILE, TILE), TILE)
        qv[rows, :] = _silu(q_ref[rows, :])
        for d, z_ref in enumerate((ff_ref, fb_ref)):
            z = z_ref[rows, :]
            lf = _logaddexp(llb, lrest + _log_sigmoid(z))
            kk_refs[d][rows, :] = one_minus_lb * _sigmoid(-z)
            cum = _split_dot(pair_bf[d], lf)
            cum_refs[d][rows, :] = cum
            min_tot = jnp.minimum(min_tot, jnp.min(chunk_total(cum, d), axis=0))
        return min_tot

    min_tot = lax.fori_loop(0, n_tiles, gates, jnp.zeros((1, HP * K), f32))
    fast_ok = jnp.min(min_tot) > HGRN_FAST_MIN_TOTAL
    for st in st_refs:
        st[...] = jnp.zeros(st.shape, f32)

    def chunk_of(k, d):
        if d == 0:
            return k
        return jnp.where(k < n_ctx_chunks, n_ctx_chunks - 1 - k, n_chunks + n_ctx_chunks - 1 - k)

    def intra(j, carry):
        rows = pl.ds(pl.multiple_of(j * TILE, TILE), TILE)
        for d in (0, 1):
            cum2 = cum_refs[d][rows, :]
            tot4 = chunk_total(cum2, d)
            et_refs[d][pl.ds(pl.multiple_of(j * CPT, CPT), CPT), :, :] = jnp.broadcast_to(
                jnp.exp(tot4), (CPT, 8, HP * K))
            r2 = jnp.broadcast_to(0.5 * tot4, (CPT, CHUNK, HP * K)).reshape(TILE, HP * K)
            e1 = jnp.exp(cum2 - r2)
            e2 = jnp.exp(r2 - cum2)
            er = jnp.exp(r2)
            qa = qv[rows, :] * e1
            kb = kk_refs[d][rows, :] * e2
            qs_refs[d][rows, :] = (qa * er).astype(bf16)
            kd_refs[d][rows, :] = (kb * er).astype(bf16)
            qa = qa.astype(bf16)
            kb = kb.astype(bf16)
            for h in range(HP):
                lanes = slice(K * h, K * (h + 1))
                a = jnp.where(pair_ok[d], _dot_nt(qa[:, lanes], kb[:, lanes]), 0.0)
                y_refs[d][rows, lanes] = _dot(a.astype(bf16), i_ref[rows, lanes].astype(bf16))
        return carry

    def carry_state(k, carry):
        for d in (0, 1):
            c = chunk_of(k, d)
            rows = pl.ds(pl.multiple_of(c * CHUNK, CHUNK), CHUNK)
            et = et_refs[d][c]
            for h in range(HP):
                lanes = slice(K * h, K * (h + 1))
                st = st_refs[2 * h + d]
                s_old = st[...]
                y_refs[d][rows, lanes] += _dot_nt(qs_refs[d][rows, lanes], s_old.astype(bf16))
                st[...] = s_old * et[0:1, lanes] + _dot_tn(i_ref[rows, lanes].astype(bf16), kd_refs[d][rows, lanes])
        return carry

    @pl.when(fast_ok)
    def _():
        lax.fori_loop(0, n_tiles, intra, 0)
        lax.fori_loop(0, n_chunks, carry_state, 0, unroll=HGRN_UNROLL)

    tril = {d: _tri(CHUNK, d == 1) for d in (0, 1)}
    rows_k = lax.broadcasted_iota(jnp.int32, (CHUNK, K), 0)
    cols_c = lax.broadcasted_iota(jnp.int32, (CHUNK, CHUNK), 1)

    def exact_scores(q, kk, cum):
        def col(s, acc):
            sel = rows_k == s
            cs = jnp.sum(jnp.where(sel, cum, 0.0), axis=0, keepdims=True)
            ks = jnp.sum(jnp.where(sel, kk, 0.0), axis=0, keepdims=True)
            w = jnp.exp(jnp.minimum(cum - cs, 0.0))
            return jnp.where(cols_c == s, jnp.sum(q * ks * w, axis=1, keepdims=True), acc)
        return lax.fori_loop(0, CHUNK, col, jnp.zeros((CHUNK, CHUNK), f32))

    def exact_step(k, carry):
        for d in (0, 1):
            c = chunk_of(k, d)
            rows = pl.ds(pl.multiple_of(c * CHUNK, CHUNK), CHUNK)
            for h in range(HP):
                lanes = slice(K * h, K * (h + 1))
                cum = cum_refs[d][rows, lanes]
                kk = kk_refs[d][rows, lanes]
                q = qv[rows, lanes]
                v = i_ref[rows, lanes].astype(bf16)
                tot = cum[0:1, :] if d == 1 else cum[CHUNK - 1:CHUNK, :]
                a = jnp.where(tril[d], exact_scores(q, kk, cum), 0.0)
                st = st_refs[2 * h + d]
                s_old = st[...]
                y_refs[d][rows, lanes] = _dot(a.astype(bf16), v) + _dot_nt(
                    (q * jnp.exp(cum)).astype(bf16), s_old.astype(bf16))
                st[...] = s_old * jnp.exp(tot) + _dot_tn(v, (kk * jnp.exp(tot - cum)).astype(bf16))
        return carry

    @pl.when(jnp.logical_not(fast_ok))
    def _():
        lax.fori_loop(0, n_chunks, exact_step, 0)

    gain = gain_ref[...]

    def readout(c, carry):
        rows = pl.ds(pl.multiple_of(c * ROW_TILE, ROW_TILE), ROW_TILE)
        o = yf[rows, :] + yb[rows, :]
        gate = _silu(g_ref[rows, :])
        y_ref[rows, :] = (jnp.concatenate(
            [_rms(o[:, K * h:K * (h + 1)], gain[:, K * h:K * (h + 1)]) for h in range(HP)], axis=1) * gate
        ).astype(y_ref.dtype)
        return carry

    lax.fori_loop(0, T // ROW_TILE, readout, 0)


def _hgrn(proj, log_lb, log_rest, gain, n_ctx):
    B, T, _ = proj.shape
    wblk = 2 * HGRN_HEAD_DIM
    col = lambda base: pl.BlockSpec((None, T, wblk), lambda b, p: (b, 0, base + p))
    par = pl.BlockSpec((1, wblk), lambda b, p: (0, p))
    K = HGRN_HEAD_DIM
    return pl.pallas_call(
        functools.partial(_hgrn_kernel, n_ctx=n_ctx),
        grid=(B, HGRN_HEADS // 2),
        in_specs=[col(b0) for b0 in HGRN_BLK] + [par, par, par],
        out_specs=pl.BlockSpec((None, T, wblk), lambda b, p: (b, 0, p)),
        out_shape=jax.ShapeDtypeStruct((B, T, BRANCH_WIDTH), bf16),
        scratch_shapes=([pltpu.VMEM((T, wblk), f32)] * 7 + [pltpu.VMEM((T, wblk), bf16)] * 4
                        + [pltpu.VMEM((T // CHUNK, 8, wblk), f32)] * 2 + [pltpu.VMEM((K, K), f32)] * 4),
        compiler_params=_cparams(2), name="hgrn2",
    )(proj, proj, proj, proj, proj, log_lb, log_rest, gain)


def _ssd_kernel(z_ref, x_ref, b_ref, c_ref, dt_ref, cwx_ref, cbx_ref, cwb_ref, cbb_ref, cwc_ref, cbc_ref,
                dtbias_ref, arow_ref, skip_ref, gain_ref, y_ref,
                xstage, bstage, cstage, xs, bs, cs, xd0, xd1, cum0, cum1, dec0, dec1, yf, yb, st0, st1, *, n_ctx):
    T = x_ref.shape[0]
    P = SSD_HEAD_DIM
    R = SSD_HPG
    XW = R * P
    TILE = SCAN_CHUNK
    n_chunks = T // CHUNK
    n_ctx_chunks = n_ctx // CHUNK
    y_refs = (yf, yb)
    st_refs = (st0, st1)
    xd_refs, cum_refs, dec_refs = (xd0, xd1), (cum0, cum1), (dec0, dec1)

    rt = lax.broadcasted_iota(jnp.int32, (TILE, TILE), 0)
    ct = lax.broadcasted_iota(jnp.int32, (TILE, TILE), 1)
    same_chunk = (rt // CHUNK) == (ct // CHUNK)
    pair_bf = {0: (same_chunk & (ct <= rt)).astype(bf16), 1: (same_chunk & (ct >= rt)).astype(bf16)}
    r4 = lax.broadcasted_iota(jnp.int32, (TILE, XW), 0) % CHUNK
    c4 = lax.broadcasted_iota(jnp.int32, (TILE, XW), 1) % CHUNK
    incl4 = {0: c4 <= r4, 1: c4 >= r4}
    strict4 = {0: r4 > c4, 1: r4 < c4}

    _stage_segments(xstage, x_ref, 0, XW, n_ctx, T)
    _stage_segments(bstage, b_ref, 0, SSD_STATE, n_ctx, T)
    _stage_segments(cstage, c_ref, 0, SSD_STATE, n_ctx, T)
    cwx, cbx = cwx_ref[...], cbx_ref[...]
    cwb, cbb = cwb_ref[...], cbb_ref[...]
    cwc, cbc = cwc_ref[...], cbc_ref[...]
    dtbias, arow = dtbias_ref[...], arow_ref[...]
    src = lax.broadcasted_iota(jnp.int32, (128, 2 * XW), 0)
    dst = lax.broadcasted_iota(jnp.int32, (128, 2 * XW), 1)
    spread = (src == R * (dst // XW) + (dst % XW) // P).astype(bf16)

    def prologue(j, carry):
        off = _staged_window(j)
        rows = pl.ds(pl.multiple_of(j * SCAN_CHUNK, SCAN_CHUNK), SCAN_CHUNK)
        x = _silu(_conv_chunk(xstage, off, cwx, cbx))
        xs[rows, :] = x
        bs[rows, :] = _silu(_conv_chunk(bstage, off, cwb, cbb)).astype(bf16)
        cs[rows, :] = _silu(_conv_chunk(cstage, off, cwc, cbc)).astype(bf16)
        dt_all = _split_dot_right(_softplus(dt_ref[rows, :] + dtbias), spread)
        for d in (0, 1):
            dt = dt_all[:, XW * d:XW * (d + 1)]
            da = dt * arow[:, XW * d:XW * (d + 1)]
            xd_refs[d][rows, :] = x * dt
            cum_refs[d][rows, :] = _split_dot(pair_bf[d], da)
            logdec = _split_dot(pair_bf[d], jnp.where(strict4[d], da, 0.0))
            dec_refs[d][rows, :] = jnp.where(incl4[d], jnp.exp(logdec), 0.0)
        return carry

    lax.fori_loop(0, T // SCAN_CHUNK, prologue, 0)
    for st in st_refs:
        st[...] = jnp.zeros(st.shape, f32)

    def chunk_of(k, d):
        if d == 0:
            return k
        return jnp.where(k < n_ctx_chunks, n_ctx_chunks - 1 - k, n_chunks + n_ctx_chunks - 1 - k)

    same_head = (lax.broadcasted_iota(jnp.int32, (XW, XW), 0) // P
                 == lax.broadcasted_iota(jnp.int32, (XW, XW), 1) // P)

    def step(k, d):
        c = chunk_of(k, d)
        rows = pl.ds(pl.multiple_of(c * CHUNK, CHUNK), CHUNK)
        cm = cs[rows, :]
        bm = bs[rows, :]
        xdt = xd_refs[d][rows, :]
        cum = cum_refs[d][rows, :]
        tot = cum[0:1, :] if d == 1 else cum[CHUNK - 1:CHUNK, :]
        cb = _dot_nt(cm, jnp.concatenate([bm] * R, axis=0))
        x_heads = jnp.where(same_head, jnp.concatenate([xdt] * R, axis=0), 0.0)
        intra = _dot((cb * dec_refs[d][rows, :]).astype(bf16), x_heads.astype(bf16))
        st = st_refs[d]
        s_old = st[...]
        inter = _dot(cm, s_old.astype(bf16)) * jnp.exp(cum)
        y_refs[d][rows, :] = intra + inter
        st[...] = s_old * jnp.exp(tot) + _dot_tn(bm, (xdt * jnp.exp(tot - cum)).astype(bf16))

    def body(k, carry):
        step(k, 0)
        step(k, 1)
        return carry

    lax.fori_loop(0, n_chunks, body, 0, unroll=SSD_UNROLL)

    skip, gain = skip_ref[...], gain_ref[...]

    def readout(c, carry):
        rows = pl.ds(pl.multiple_of(c * ROW_TILE, ROW_TILE), ROW_TILE)
        y = skip * xs[rows, :] + yf[rows, :] + yb[rows, :]
        y_ref[rows, :] = _rms(y * _silu(z_ref[rows, :]), gain).astype(y_ref.dtype)
        return carry

    lax.fori_loop(0, T // ROW_TILE, readout, 0)


def _ssd(proj, cwx, cbx, cwb, cbb, cwc, cbc, dtbias, arow, skip, gain, n_ctx):
    B, T, _ = proj.shape
    XW = SSD_HPG * SSD_HEAD_DIM
    N = SSD_STATE
    col = lambda base, w: pl.BlockSpec((None, T, w), lambda b, g: (b, 0, base + g))
    par = lambda r, w: pl.BlockSpec((None, r, w), lambda b, g: (g, 0, 0))
    return pl.pallas_call(
        functools.partial(_ssd_kernel, n_ctx=n_ctx),
        grid=(B, SSD_GROUPS),
        in_specs=[col(SSD_Z_BLK, XW), col(SSD_X_BLK, XW), col(SSD_B_BLK, N), col(SSD_C_BLK, N), col(SSD_DT_BLK, 128),
                  par(CONV_WIDTH, XW), par(1, XW), par(CONV_WIDTH, N), par(1, N), par(CONV_WIDTH, N), par(1, N),
                  par(1, 128), par(1, 2 * XW), par(1, XW), par(1, XW)],
        out_specs=pl.BlockSpec((None, T, XW), lambda b, g: (b, 0, g)),
        out_shape=jax.ShapeDtypeStruct((B, T, SSD_WIDTH), bf16),
        scratch_shapes=[pltpu.VMEM((T + 3 * SEG_PAD, XW), f32), pltpu.VMEM((T + 3 * SEG_PAD, N), f32),
                        pltpu.VMEM((T + 3 * SEG_PAD, N), f32),
                        pltpu.VMEM((T, XW), f32), pltpu.VMEM((T, N), bf16), pltpu.VMEM((T, N), bf16),
                        pltpu.VMEM((T, XW), f32), pltpu.VMEM((T, XW), f32), pltpu.VMEM((T, XW), f32),
                        pltpu.VMEM((T, XW), f32), pltpu.VMEM((T, XW), f32), pltpu.VMEM((T, XW), f32),
                        pltpu.VMEM((T, XW), f32), pltpu.VMEM((T, XW), f32),
                        pltpu.VMEM((N, XW), f32), pltpu.VMEM((N, XW), f32)],
        compiler_params=_cparams(2), name="ssd",
    )(proj, proj, proj, proj, proj, cwx, cbx, cwb, cbb, cwc, cbc, dtbias, arow, skip, gain)


def _post_kernel(x_ref, gl_ref, ya_ref, yb_ref, yc_ref, yd_ref, m_ref, wbr_ref, wout_ref, wup_ref, wdn_ref,
                 g1_ref, g2_ref, g3_ref, *rest, emit_next):
    D = D_MODEL
    ys = (ya_ref, yb_ref, yc_ref, yd_ref)
    merged = None
    for i in range(N_BRANCH):
        t = _sigmoid(gl_ref[:, D * i:D * (i + 1)]) * _dot(ys[i][...], wbr_ref[i])
        merged = t if merged is None else merged + t
    mix = _dot(merged.astype(bf16), wout_ref[...])
    mod = lambda k: m_ref[:, D * k:D * (k + 1)]
    x1 = x_ref[...] + mod(2) * _rms(mix, g1_ref[...])
    h2 = (_rms(x1, g2_ref[...]) * (1.0 + mod(4)) + mod(3)).astype(bf16)
    down = None
    hc = 1024
    for c in range(MLP_HIDDEN // hc):
        u = jnp.maximum(_dot(h2, wup_ref[:, hc * c:hc * (c + 1)]), 0.0)
        t = _dot((u * u).astype(bf16), wdn_ref[hc * c:hc * (c + 1), :])
        down = t if down is None else down + t
    x2 = x1 + mod(5) * _rms(down, g3_ref[...])
    if emit_next:
        mn_ref, gn_ref, o_ref, h_ref = rest
        h_ref[...] = _pre_norm(x2, mn_ref, gn_ref)
    else:
        (o_ref,) = rest
    o_ref[...] = x2


def _post(xs, proj, ya, yb, yc, yd, mods_l, wbr, wout, wup, wdn, g1, g2, g3, nxt):
    B, T, D = xs.shape
    W = BRANCH_WIDTH
    nt = T // ROW_TILE
    first = 1 if nxt is None else 0
    tile = lambda w: pl.BlockSpec((None, ROW_TILE, w), lambda b, i: (b, i + first, 0))
    once = lambda *s: pl.BlockSpec(s, lambda b, i: (0,) * len(s), pipeline_mode=pl.Buffered(1))
    mod_idx = _mod_row_index(B)
    mod_row = pl.BlockSpec((None, 1, 6 * D), lambda b, i: mod_idx(b, i + first))
    out_tile = pl.BlockSpec((None, ROW_TILE, D), lambda b, i: (b, i, 0))
    in_specs = [tile(D), tile(MERGE_COLS), tile(W), tile(W), tile(W), tile(W), mod_row,
                once(N_BRANCH, W, D), once(D, D), once(D, MLP_HIDDEN), once(MLP_HIDDEN, D),
                once(1, D), once(1, D), once(1, D)]
    args = [xs, proj, ya, yb, yc, yd, mods_l, wbr, wout, wup, wdn, g1, g2, g3]
    out_specs, out_shape = out_tile, jax.ShapeDtypeStruct((B, T - first * ROW_TILE, D), f32)
    if nxt is not None:
        in_specs += [mod_row, once(1, D)]
        args += list(nxt)
        out_specs = [out_tile, out_tile]
        out_shape = [out_shape, jax.ShapeDtypeStruct((B, T, D), bf16)]
    return pl.pallas_call(
        functools.partial(_post_kernel, emit_next=nxt is not None),
        grid=(B, nt - first), in_specs=in_specs, out_specs=out_specs, out_shape=out_shape,
        compiler_params=_cparams(2), name="merge_mlp",
    )(*args)


def _block_diag(w):
    eye = jnp.eye(LRU_BLOCKS, dtype=w.dtype)
    return jnp.einsum('hij,hg->higj', w, eye).reshape(LRU_WIDTH, LRU_WIDTH)


def _rope_tables(n_lat):
    half = ATTN_HEAD_DIM // 2
    quarter = half // 2
    inv_freq = ROPE_BASE ** (-jnp.arange(quarter, dtype=f32) / quarter)
    t = jnp.arange(n_lat, dtype=jnp.int32)
    rows = (t // GRID_W).astype(f32)[:, None] * inv_freq
    cols = (t % GRID_W).astype(f32)[:, None] * inv_freq
    cos_h = jnp.concatenate([jnp.cos(rows), jnp.cos(rows), jnp.cos(cols), jnp.cos(cols)], axis=-1)
    sin_h = jnp.concatenate([-jnp.sin(rows), jnp.sin(rows), -jnp.sin(cols), jnp.sin(cols)], axis=-1)
    return jnp.tile(cos_h, (1, 2)), jnp.tile(sin_h, (1, 2))


def kernel(x, c, ctx, c_ctx, w_ada, b_ada, g_pre_mix, g_post_mix, g_pre_mlp, g_post_mlp, w_in, lru_conv_w, lru_conv_b, lru_rec_w, lru_rec_b, lru_inp_w, lru_inp_b, lru_lambda, attn_sink, hgrn_lb_logits, hgrn_norm_g, ssd_conv_w, ssd_conv_b, ssd_dt_bias, ssd_a_log, ssd_skip, ssd_norm_g, w_branch, w_out, w_mlp_up, w_mlp_down):
    B, L, D = x.shape
    n_ctx = ctx.shape[1]
    assert D == D_MODEL and n_ctx == ROW_TILE == SCAN_CHUNK and L % ROW_TILE == 0 and L % GRID_W == 0
    assert w_in.shape == (DEPTH, D_MODEL, IN_COLS)

    pad_rows = (-(B + 1)) % 8
    c_all = jnp.concatenate([c, c_ctx[None, :], jnp.zeros((pad_rows, D), f32)], axis=0)
    mods = _ada(c_all, w_ada, b_ada)
    mods = mods.reshape(DEPTH, c_all.shape[0], 1, 6 * D)

    lb = jnp.cumsum(jax.nn.softmax(hgrn_lb_logits.astype(f32), axis=0), axis=0)
    lb = lb - lb[0]
    cos_t, sin_t = _rope_tables(L)
    G, R, P, N = SSD_GROUPS, SSD_HPG, SSD_HEAD_DIM, SSD_STATE

    for l in range(DEPTH):
        last = l == DEPTH - 1
        wl = w_in[l]
        dtf, dtb = wl[:, 5888:5896], wl[:, 5896:5904]
        zpad = jnp.zeros((D, 128 - 2 * R), f32)
        dt_cols = [jnp.concatenate([dtf[:, R * g:R * (g + 1)], dtb[:, R * g:R * (g + 1)], zpad], axis=1)
                   for g in range(G)]
        w_perm = jnp.concatenate([wl[:, 5904:], wl[:, :5888]] + dt_cols, axis=1).astype(bf16)

        if l == 0:
            xs, h = _first_norm(ctx, x, mods[0], g_pre_mix[0][None, :])
        proj = _in_proj(h, w_perm)

        wbd = jnp.concatenate([_block_diag(lru_rec_w[l, 0]), _block_diag(lru_inp_w[l, 0]),
                               _block_diag(lru_rec_w[l, 1]), _block_diag(lru_inp_w[l, 1])], axis=1).astype(bf16)
        gb = jnp.concatenate([lru_rec_b[l, 0], lru_inp_b[l, 0], lru_rec_b[l, 1], lru_inp_b[l, 1]])[None, :]
        csp = LRU_C * jax.nn.softplus(-lru_lambda[l])
        ya = _lru(proj, lru_conv_w[l], lru_conv_b[l][None, :], wbd, gb, csp, n_ctx)

        yb = _attn(proj, attn_sink[l], cos_t, sin_t, n_ctx)

        lbl = lb[l][None, :]
        yc = _hgrn(proj, jnp.log(lbl), jnp.log1p(-lbl), hgrn_norm_g[l][None, :], n_ctx)

        cw, cbv = ssd_conv_w[l], ssd_conv_b[l]
        XW = R * P
        grp = lambda a, off, w: jnp.stack([a[..., off + w * g:off + w * (g + 1)] for g in range(G)], axis=0)
        cwx, cbx = grp(cw, 0, XW), grp(cbv[None, :], 0, XW)
        cwb, cbb = grp(cw, SSD_WIDTH, N), grp(cbv[None, :], SSD_WIDTH, N)
        cwc, cbc = grp(cw, SSD_WIDTH + G * N, N), grp(cbv[None, :], SSD_WIDTH + G * N, N)
        a_neg = -jnp.exp(ssd_a_log[l].astype(f32))
        per_group = lambda v: [jnp.concatenate([v[0, R * g:R * (g + 1)], v[1, R * g:R * (g + 1)]]) for g in range(G)]
        lanes_compact = lambda v: jnp.stack([jnp.pad(u, (0, 128 - 2 * R))[None, :] for u in per_group(v)], axis=0)
        lanes_spread = lambda v: jnp.stack([jnp.repeat(u, P)[None, :] for u in per_group(v)], axis=0)
        skip = jnp.repeat(ssd_skip[l], P).reshape(G, 1, XW)
        gain_d = ssd_norm_g[l].reshape(G, 1, XW)
        yd = _ssd(proj, cwx, cbx, cwb, cbb, cwc, cbc, lanes_compact(ssd_dt_bias[l]), lanes_spread(a_neg), skip, gain_d, n_ctx)

        nxt = None if last else (mods[l + 1], g_pre_mix[l + 1][None, :])
        out = _post(xs, proj, ya, yb, yc, yd, mods[l], w_branch[l].astype(bf16), w_out[l].astype(bf16),
                    w_mlp_up[l].astype(bf16), w_mlp_down[l].astype(bf16),
                    g_post_mix[l][None, :], g_pre_mlp[l][None, :], g_post_mlp[l][None, :], nxt)
        if last:
            return out
        xs, h = out
```

```python
import functools

import numpy as np
import jax
import jax.numpy as jnp
from jax import lax
from jax.experimental import pallas as pl
from jax.experimental.pallas import tpu as pltpu

f32 = jnp.float32
bf16 = jnp.bfloat16

D_MODEL = 1024
DEPTH = 2
GRID_W = 64
N_BRANCH = 4
BRANCH_WIDTH = D_MODEL // 2
CONV_WIDTH = 4
LRU_WIDTH = BRANCH_WIDTH
LRU_BLOCKS = 8
LRU_BLOCK_W = LRU_WIDTH // LRU_BLOCKS
LRU_C = 8.0
ATTN_HEAD_DIM = 64
ATTN_HEADS = BRANCH_WIDTH // ATTN_HEAD_DIM
ATTN_KV_HEADS = 2
ATTN_REP = ATTN_HEADS // ATTN_KV_HEADS
ATTN_WINDOW = 128
ATTN_BLOCK = 128
ROPE_BASE = 10000.0
HGRN_HEAD_DIM = 128
HGRN_HEADS = BRANCH_WIDTH // HGRN_HEAD_DIM
SSD_WIDTH = BRANCH_WIDTH
SSD_HEAD_DIM = 64
SSD_HEADS = SSD_WIDTH // SSD_HEAD_DIM
SSD_GROUPS = 2
SSD_HPG = SSD_HEADS // SSD_GROUPS
SSD_STATE = 128
MLP_HIDDEN = 4 * D_MODEL
NORM_EPS = 1e-6

IN_COLS = 10000
MERGE_COLS = N_BRANCH * D_MODEL
PROJ_COLS = 10240
LRU_BLK = 4
ATTN_Q_BLK = 10
ATTN_KV_BLK = 22
HGRN_BLK = (23, 25, 27, 29, 31)
SSD_Z_BLK = 33
SSD_X_BLK = 35
SSD_B_BLK = 74
SSD_C_BLK = 76
SSD_DT_BLK = 78

ROW_TILE = 256
SCAN_CHUNK = 256
CHUNK = 64
SEG_PAD = 8
LRU_U_PITCH = 56
LRU_G_PITCH = 40
HGRN_FAST_MIN_TOTAL = -160.0
HGRN_UNROLL = 12
SSD_UNROLL = 12
MASK_BIAS = -1e30
VMEM_LIMIT = 58 * 1024 * 1024


def _cparams(n_axes):
    return pltpu.CompilerParams(dimension_semantics=("arbitrary",) * n_axes, vmem_limit_bytes=VMEM_LIMIT)


def _sigmoid(x):
    return 0.5 * jnp.tanh(0.5 * x) + 0.5


def _silu(x):
    return x * jax.nn.sigmoid(x)


def _gelu_tanh(x):
    return 0.5 * x * (1.0 + jnp.tanh(np.float32(np.sqrt(2.0 / np.pi)) * (x + 0.044715 * (x * x * x))))


def _expm1_given_exp(x, ex):
    poly = x * (1.0 + x * (1.0 / 2) * (1.0 + x * (1.0 / 3) * (1.0 + x * (1.0 / 4))))
    return jnp.where(jnp.abs(x) < 0.03, poly, ex - 1.0)


def _sqrt_nonneg(y):
    return jnp.where(y > 0.0, y * lax.rsqrt(y), 0.0)


def _log_sigmoid(x):
    return jnp.minimum(x, 0.0) - jnp.log(1.0 + jnp.exp(-jnp.abs(x)))


def _logaddexp(a, b):
    return jnp.maximum(a, b) + jnp.log(1.0 + jnp.exp(-jnp.abs(a - b)))


def _softplus(x):
    return jnp.maximum(x, 0.0) + jnp.log1p(jnp.exp(-jnp.abs(x)))


def _rms(x, g):
    return x * lax.rsqrt(jnp.mean(x * x, axis=-1, keepdims=True) + NORM_EPS) * g


def _dot(a, b):
    return jnp.dot(a, b, preferred_element_type=f32)


def _dot_nt(a, b):
    return lax.dot_general(a, b, (((1,), (1,)), ((), ())), preferred_element_type=f32)


def _dot_tn(a, b):
    return lax.dot_general(a, b, (((0,), (0,)), ((), ())), preferred_element_type=f32)


def _split_dot(tri, x):
    hi = x.astype(bf16)
    lo = (x - hi.astype(f32)).astype(bf16)
    return _dot(tri, hi) + _dot(tri, lo)


def _split_dot_right(x, sel):
    hi = x.astype(bf16)
    lo = (x - hi.astype(f32)).astype(bf16)
    return _dot(hi, sel) + _dot(lo, sel)


def _tri(n, reverse):
    r = lax.broadcasted_iota(jnp.int32, (n, n), 0)
    c = lax.broadcasted_iota(jnp.int32, (n, n), 1)
    return (c >= r) if reverse else (c <= r)


def _ada_kernel(c_ref, w_ref, b_ref, o_ref):
    cond = _silu(c_ref[...])
    o_ref[...] = jnp.dot(cond, w_ref[...], preferred_element_type=f32,
                         precision=lax.Precision.HIGHEST) + b_ref[...]


def _ada(c_all, w_ada, b_ada):
    rows = c_all.shape[0]
    nt = 6 * D_MODEL // 1024
    return pl.pallas_call(
        _ada_kernel,
        grid=(DEPTH, nt),
        in_specs=[pl.BlockSpec((rows, D_MODEL), lambda l, j: (0, 0)),
                  pl.BlockSpec((None, D_MODEL, 1024), lambda l, j: (l, 0, j)),
                  pl.BlockSpec((None, 1, 1024), lambda l, j: (l, 0, j))],
        out_specs=pl.BlockSpec((None, rows, 1024), lambda l, j: (l, 0, j)),
        out_shape=jax.ShapeDtypeStruct((DEPTH, rows, 6 * D_MODEL), f32),
        compiler_params=_cparams(2), name="ada_mod",
    )(c_all, w_ada, b_ada.reshape(DEPTH, 1, 6 * D_MODEL))


def _mod_row_index(n_batch):
    return lambda b, i: (jnp.where(i == 0, n_batch, b), 0, 0)


def _pre_norm(x, m_ref, g_ref):
    shift = m_ref[:, 0:D_MODEL]
    scale = m_ref[:, D_MODEL:2 * D_MODEL]
    return (_rms(x, g_ref[...]) * (1.0 + scale) + shift).astype(bf16)


def _proj_kernel(a_ref, w_ref, o_ref):
    o_ref[...] = _dot(a_ref[...], w_ref[...])


def _first_proj_kernel(ctx_ref, x_ref, mc_ref, ml_ref, g_ref, w_ref, o_ref, h_ref):
    n_ctx = ctx_ref.shape[0]

    @pl.when(pl.program_id(1) == 0)
    def _():
        h_ref[0:n_ctx, :] = _pre_norm(ctx_ref[...], mc_ref, g_ref)
        h_ref[n_ctx:, :] = _pre_norm(x_ref[...], ml_ref, g_ref)

    o_ref[...] = _dot(h_ref[...], w_ref[...])


def _first_in_proj(ctx, x, mods_l, g, w_perm):
    B, L, D = x.shape
    n_ctx = ctx.shape[1]
    T = n_ctx + L
    tn = 1024
    return pl.pallas_call(
        _first_proj_kernel,
        grid=(B, PROJ_COLS // tn),
        in_specs=[pl.BlockSpec((None, n_ctx, D), lambda b, j: (b, 0, 0)),
                  pl.BlockSpec((None, L, D), lambda b, j: (b, 0, 0)),
                  pl.BlockSpec((None, 1, 6 * D), lambda b, j: (B, 0, 0)),
                  pl.BlockSpec((None, 1, 6 * D), lambda b, j: (b, 0, 0)),
                  pl.BlockSpec((1, D), lambda b, j: (0, 0)),
                  pl.BlockSpec((D, tn), lambda b, j: (0, j))],
        out_specs=pl.BlockSpec((None, T, tn), lambda b, j: (b, 0, j)),
        out_shape=jax.ShapeDtypeStruct((B, T, PROJ_COLS), f32),
        scratch_shapes=[pltpu.VMEM((T, D), bf16)],
        compiler_params=_cparams(2), name="first_in_proj",
    )(ctx, x, mods_l, mods_l, g, w_perm)


def _in_proj(h, w_perm):
    B, T, _ = h.shape
    tn = 1024
    return pl.pallas_call(
        _proj_kernel,
        grid=(B, PROJ_COLS // tn),
        in_specs=[pl.BlockSpec((None, T, D_MODEL), lambda b, j: (b, 0, 0)),
                  pl.BlockSpec((D_MODEL, tn), lambda b, j: (0, j))],
        out_specs=pl.BlockSpec((None, T, tn), lambda b, j: (b, 0, j)),
        out_shape=jax.ShapeDtypeStruct((B, T, PROJ_COLS), f32),
        compiler_params=_cparams(2), name="in_proj",
    )(h, w_perm)


def _stage_segments(dst, src_ref, col0, width, n_ctx, t_all):
    z = jnp.zeros((SEG_PAD, width), f32)
    dst[0:SEG_PAD, :] = z
    dst[SEG_PAD:SEG_PAD + n_ctx, :] = src_ref[0:n_ctx, col0:col0 + width]
    dst[SEG_PAD + n_ctx:2 * SEG_PAD + n_ctx, :] = z
    dst[2 * SEG_PAD + n_ctx:2 * SEG_PAD + t_all, :] = src_ref[n_ctx:t_all, col0:col0 + width]
    dst[2 * SEG_PAD + t_all:3 * SEG_PAD + t_all, :] = z


def _staged_window(j):
    return pl.multiple_of(SCAN_CHUNK * j + jnp.where(j >= 1, SEG_PAD, 0), SEG_PAD)


def _conv_chunk(stage_ref, win0, cw, cb):
    n = SCAN_CHUNK + 2 * SEG_PAD
    win = stage_ref[pl.ds(win0, n), :]
    acc = cb + cw[2:3, :] * win[SEG_PAD:SEG_PAD + SCAN_CHUNK]
    for tap, o in ((0, -2), (1, -1), (3, 1)):
        acc = acc + cw[tap:tap + 1, :] * pltpu.roll(win, (-o) % n, 0)[SEG_PAD:SEG_PAD + SCAN_CHUNK]
    return acc


def _lru_kernel(p_ref, cw_ref, cb_ref, wbd_ref, gb_ref, csp_ref, y_ref, stage, useg, gseg, hfwd, *, n_ctx):
    T = p_ref.shape[0]
    W = LRU_WIDTH
    SEG = SCAN_CHUNK // 8
    n_tiles = W // 128
    n_chunks = T // SCAN_CHUNK
    for c in range(n_tiles):
        _stage_segments(stage.at[c], p_ref, 128 * c, 128, n_ctx, T)
        for j in range(n_chunks):
            win0 = SCAN_CHUNK * j + (SEG_PAD if j >= 1 else 0)
            for s in range(8):
                slot = 8 * j + s
                useg[c, LRU_U_PITCH * slot:LRU_U_PITCH * slot + SEG + 2 * SEG_PAD, :] = stage[
                    c, win0 + SEG * s:win0 + SEG * (s + 1) + 2 * SEG_PAD, :]
                gseg[c, LRU_G_PITCH * slot:LRU_G_PITCH * slot + SEG, :] = p_ref[
                    SCAN_CHUNK * j + SEG * s:SCAN_CHUNK * j + SEG * (s + 1), W + 128 * c:W + 128 * (c + 1)]
    cw = cw_ref[...]
    cb = cb_ref[...]

    def segment_rows(ref, row0, pitch):
        return jnp.concatenate([ref[c, pl.ds(row0, 8, stride=pitch), :] for c in range(n_tiles)], axis=1)

    def chunk(j, carry, d):
        reverse = d == 1
        u0 = pl.multiple_of(j * (8 * LRU_U_PITCH), 8) + SEG_PAD
        v = {k: segment_rows(useg, u0 + k, LRU_U_PITCH) for k in range(-2, SEG + 1)}
        x = jnp.concatenate(
            [cb + cw[0:1, :] * v[k - 2] + cw[1:2, :] * v[k - 1] + cw[2:3, :] * v[k] + cw[3:4, :] * v[k + 1]
             for k in range(SEG)], axis=0)
        g = _dot(x.astype(bf16), wbd_ref[:, 2 * W * d:2 * W * (d + 1)]) + gb_ref[:, 2 * W * d:2 * W * (d + 1)]
        r = _sigmoid(g[:, 0:W])
        i = _sigmoid(g[:, W:2 * W])
        log_a = -csp_ref[d:d + 1, :] * r
        a = jnp.exp(log_a)
        b = _sqrt_nonneg(-_expm1_given_exp(2.0 * log_a, a * a)) * (i * x)
        order = range(SEG - 1, -1, -1) if reverse else range(SEG)
        hl, pk = [None] * SEG, [None] * SEG
        h = p = None
        for k in order:
            ak, bk = a[8 * k:8 * k + 8], b[8 * k:8 * k + 8]
            h = bk if h is None else ak * h + bk
            p = ak if p is None else ak * p
            hl[k], pk[k] = h, p
        seg_in = [None] * 8
        for s in (range(7, -1, -1) if reverse else range(8)):
            seg_in[s] = carry
            carry = h[s:s + 1, :] + p[s:s + 1, :] * carry
        seg_in = jnp.concatenate(seg_in, axis=0)
        return [hl[k] + pk[k] * seg_in for k in range(SEG)], carry

    zero = jnp.zeros((1, W), f32)

    def fwd_body(j, carry):
        hs, carry = chunk(j, carry, 0)
        base = pl.multiple_of(j * SCAN_CHUNK, SCAN_CHUNK)
        for k in range(SEG):
            hfwd[pl.ds(base + 8 * k, 8), :] = hs[k]
        return carry

    lax.fori_loop(0, n_chunks, fwd_body, zero)

    def bwd_body(n, carry):
        j = jnp.where(n == 0, 0, n_chunks - n)
        hs, carry = chunk(j, carry, 1)
        base = pl.multiple_of(j * SCAN_CHUNK, SCAN_CHUNK)
        g0 = pl.multiple_of(j * (8 * LRU_G_PITCH), 8)
        for k in range(SEG):
            yk = (hfwd[pl.ds(base + 8 * k, 8), :] + hs[k]) * _gelu_tanh(segment_rows(gseg, g0 + k, LRU_G_PITCH))
            for c in range(n_tiles):
                gseg[c, pl.ds(g0 + k, 8, stride=LRU_G_PITCH), :] = yk[:, 128 * c:128 * (c + 1)]
        for c in range(n_tiles):
            for s in range(8):
                y_ref[pl.ds(base + SEG * s, SEG), 128 * c:128 * (c + 1)] = gseg[
                    c, pl.ds(g0 + LRU_G_PITCH * s, SEG), :].astype(y_ref.dtype)
        return carry

    lax.fori_loop(0, n_chunks, bwd_body, zero)


def _lru(proj, cw, cb, wbd, gb, csp, n_ctx):
    B, T, _ = proj.shape
    W = LRU_WIDTH
    full = lambda *s: pl.BlockSpec(s, lambda b: (0,) * len(s))
    return pl.pallas_call(
        functools.partial(_lru_kernel, n_ctx=n_ctx),
        grid=(B,),
        in_specs=[pl.BlockSpec((None, T, 2 * W), lambda b: (b, 0, LRU_BLK)),
                  full(CONV_WIDTH, W), full(1, W), full(W, 4 * W), full(1, 4 * W), full(2, W)],
        out_specs=pl.BlockSpec((None, T, W), lambda b: (b, 0, 0)),
        out_shape=jax.ShapeDtypeStruct((B, T, W), bf16),
        scratch_shapes=[pltpu.VMEM((W // 128, T + 3 * SEG_PAD, 128), f32),
                        pltpu.VMEM((W // 128, T // SCAN_CHUNK * 8 * LRU_U_PITCH, 128), f32),
                        pltpu.VMEM((W // 128, T // SCAN_CHUNK * 8 * LRU_G_PITCH, 128), f32),
                        pltpu.VMEM((T, W), f32)],
        compiler_params=_cparams(1), name="rglru",
    )(proj, cw, cb, wbd, gb, csp)


def _attn_kernel(sink_ref, q_ref, kv_ref, cos_ref, sin_ref, y_ref, qs, kp, vp, kc, vc, *, n_ctx):
    T = q_ref.shape[0]
    L = T - n_ctx
    hd = ATTN_HEAD_DIM
    G, R = ATTN_KV_HEADS, ATTN_REP
    nb = L // ATTN_BLOCK
    span = 3 * ATTN_BLOCK
    scale = np.float32(hd ** -0.5)

    def swap_halves(x):
        lane = lax.broadcasted_iota(jnp.int32, x.shape, 1)
        return jnp.where((lane & 31) < 16, pltpu.roll(x, 128 - 16, 1), pltpu.roll(x, 16, 1))

    zpad = jnp.zeros((ATTN_BLOCK, hd), bf16)
    for g in range(G):
        kp[g, 0:ATTN_BLOCK, :] = zpad
        kp[g, ATTN_BLOCK + L:2 * ATTN_BLOCK + L, :] = zpad
        vp[g, 0:ATTN_BLOCK, :] = zpad
        vp[g, ATTN_BLOCK + L:2 * ATTN_BLOCK + L, :] = zpad
        kc[g] = kv_ref[0:n_ctx, hd * g:hd * g + hd].astype(bf16)
        vc[g] = kv_ref[0:n_ctx, hd * (G + g):hd * (G + g) + hd].astype(bf16)

    def rope_rows(c, carry):
        r0 = pl.multiple_of(c * ATTN_BLOCK, ATTN_BLOCK)
        cos = cos_ref[pl.ds(r0, ATTN_BLOCK), :]
        sin = sin_ref[pl.ds(r0, ATTN_BLOCK), :]
        for s in range(ATTN_HEADS * hd // 128):
            xq = q_ref[pl.ds(n_ctx + r0, ATTN_BLOCK), 128 * s:128 * s + 128]
            xr = ((xq * cos + swap_halves(xq) * sin) * scale).astype(bf16)
            qs[2 * s, pl.ds(r0, ATTN_BLOCK), :] = xr[:, 0:hd]
            qs[2 * s + 1, pl.ds(r0, ATTN_BLOCK), :] = xr[:, hd:2 * hd]
        xk = kv_ref[pl.ds(n_ctx + r0, ATTN_BLOCK), 0:128]
        xr = (xk * cos + swap_halves(xk) * sin).astype(bf16)
        xv = kv_ref[pl.ds(n_ctx + r0, ATTN_BLOCK), 128:256].astype(bf16)
        for g in range(G):
            kp[g, pl.ds(ATTN_BLOCK + r0, ATTN_BLOCK), :] = xr[:, hd * g:hd * g + hd]
            vp[g, pl.ds(ATTN_BLOCK + r0, ATTN_BLOCK), :] = xv[:, hd * g:hd * g + hd]
        return carry

    lax.fori_loop(0, nb, rope_rows, 0)

    def sink_column(g, rows_per_head):
        rows = lax.broadcasted_iota(jnp.int32, (R * rows_per_head, 1), 0)
        col = jnp.full((R * rows_per_head, 1), sink_ref[R * g + R - 1], f32)
        for r in range(R - 2, -1, -1):
            col = jnp.where(rows < (r + 1) * rows_per_head, sink_ref[R * g + r], col)
        return col

    def softmax_pv(pieces, sink, v):
        blocks = [s[:, 128 * j:128 * (j + 1)] for s in pieces for j in range(s.shape[1] // 128)]
        m = functools.reduce(jnp.maximum, blocks)
        m = jnp.maximum(jnp.max(m, axis=-1, keepdims=True), sink)
        ps = [jnp.exp(s - m) for s in blocks]
        den = jnp.sum(functools.reduce(jnp.add, ps), axis=-1, keepdims=True) + jnp.exp(sink - m)
        return _dot(jnp.concatenate([p.astype(bf16) for p in ps], axis=1), v) * (1.0 / den)

    qi = lax.broadcasted_iota(jnp.int32, (R * ATTN_BLOCK, ATTN_BLOCK), 0) & (ATTN_BLOCK - 1)
    kj = lax.broadcasted_iota(jnp.int32, (R * ATTN_BLOCK, ATTN_BLOCK), 1)
    bias_a = jnp.where(kj - qi >= ATTN_BLOCK - ATTN_WINDOW, 0.0, MASK_BIAS).astype(f32)
    bias_c = jnp.where(kj - qi <= ATTN_WINDOW - ATTN_BLOCK, 0.0, MASK_BIAS).astype(f32)

    def block(n, carry):
        r0 = pl.multiple_of(n * ATTN_BLOCK, ATTN_BLOCK)
        ba = bias_a + jnp.where(n == 0, MASK_BIAS, 0.0)
        bc = bias_c + jnp.where(n == nb - 1, MASK_BIAS, 0.0)
        for g in range(G):
            q = jnp.concatenate([qs[R * g + r, pl.ds(r0, ATTN_BLOCK), :] for r in range(R)], axis=0)
            k = jnp.concatenate([kp[g, pl.ds(r0, span), :], kc[g]], axis=0)
            v = jnp.concatenate([vp[g, pl.ds(r0, span), :], vc[g]], axis=0)
            s = _dot_nt(q, k)
            pieces = [s[:, 0:ATTN_BLOCK] + ba, s[:, ATTN_BLOCK:2 * ATTN_BLOCK],
                      s[:, 2 * ATTN_BLOCK:span] + bc, s[:, span:]]
            o = softmax_pv(pieces, sink_column(g, ATTN_BLOCK), v)
            y_ref[pl.ds(n_ctx + r0, ATTN_BLOCK), R * hd * g:R * hd * (g + 1)] = jnp.concatenate(
                [o[ATTN_BLOCK * r:ATTN_BLOCK * (r + 1)] for r in range(R)], axis=1).astype(y_ref.dtype)
        return carry

    lax.fori_loop(0, nb, block, 0, unroll=2)

    for g in range(G):
        q = jnp.concatenate(
            [(q_ref[0:n_ctx, hd * (R * g + r):hd * (R * g + r + 1)] * scale).astype(bf16) for r in range(R)], axis=0)
        o = softmax_pv([_dot_nt(q, kc[g])], sink_column(g, n_ctx), vc[g])
        y_ref[0:n_ctx, R * hd * g:R * hd * (g + 1)] = jnp.concatenate(
            [o[n_ctx * r:n_ctx * (r + 1)] for r in range(R)], axis=1).astype(y_ref.dtype)


def _attn(proj, sink, cos_t, sin_t, n_ctx):
    B, T, _ = proj.shape
    L = T - n_ctx
    W = BRANCH_WIDTH
    hd = ATTN_HEAD_DIM
    return pl.pallas_call(
        functools.partial(_attn_kernel, n_ctx=n_ctx),
        grid=(B,),
        in_specs=[pl.BlockSpec(memory_space=pltpu.SMEM),
                  pl.BlockSpec((None, T, W), lambda b: (b, 0, ATTN_Q_BLK)),
                  pl.BlockSpec((None, T, 256), lambda b: (b, 0, ATTN_KV_BLK)),
                  pl.BlockSpec((L, 128), lambda b: (0, 0)),
                  pl.BlockSpec((L, 128), lambda b: (0, 0))],
        out_specs=pl.BlockSpec((None, T, W), lambda b: (b, 0, 0)),
        out_shape=jax.ShapeDtypeStruct((B, T, W), bf16),
        scratch_shapes=[pltpu.VMEM((ATTN_HEADS, L, hd), bf16),
                        pltpu.VMEM((ATTN_KV_HEADS, L + 2 * ATTN_BLOCK, hd), bf16),
                        pltpu.VMEM((ATTN_KV_HEADS, L + 2 * ATTN_BLOCK, hd), bf16),
                        pltpu.VMEM((ATTN_KV_HEADS, n_ctx, hd), bf16),
                        pltpu.VMEM((ATTN_KV_HEADS, n_ctx, hd), bf16)],
        compiler_params=_cparams(1), name="window_attn",
    )(sink, proj, proj, cos_t, sin_t)


def _hgrn_kernel(q_ref, i_ref, ff_ref, fb_ref, g_ref, llb_ref, lrest_ref, gain_ref, y_ref,
                 qv, kk0, kk1, cum0, cum1, yf, yb, qs0, qs1, kd0, kd1, et0, et1, st0, st1, st2, st3, *, n_ctx):
    T = q_ref.shape[0]
    K = HGRN_HEAD_DIM
    HP = 2
    TILE = ROW_TILE
    CPT = TILE // CHUNK
    n_tiles = T // TILE
    n_chunks = T // CHUNK
    n_ctx_chunks = n_ctx // CHUNK
    kk_refs, cum_refs, y_refs = (kk0, kk1), (cum0, cum1), (yf, yb)
    qs_refs, kd_refs, et_refs = (qs0, qs1), (kd0, kd1), (et0, et1)
    st_refs = (st0, st1, st2, st3)

    def chunk_rows(a):
        return a.reshape(CPT, CHUNK, a.shape[-1])

    def chunk_total(cum, d):
        c4 = chunk_rows(cum)
        return c4[:, 0:1, :] if d == 1 else c4[:, CHUNK - 1:CHUNK, :]

    rt = lax.broadcasted_iota(jnp.int32, (TILE, TILE), 0)
    ct = lax.broadcasted_iota(jnp.int32, (TILE, TILE), 1)
    same_chunk = (rt // CHUNK) == (ct // CHUNK)
    pair_ok = {0: same_chunk & (ct <= rt), 1: same_chunk & (ct >= rt)}
    pair_bf = {d: pair_ok[d].astype(bf16) for d in (0, 1)}

    llb = llb_ref[...]
    lrest = lrest_ref[...]
    one_minus_lb = jnp.exp(lrest)

    def gates(j, min_tot):
        rows = pl.ds(pl.multiple_of(j * TILE, TILE), TILE)
        qv[rows, :] = _silu(q_ref[rows, :])
        for d, z_ref in enumerate((ff_ref, fb_ref)):
            z = z_ref[rows, :]
            lf = _logaddexp(llb, lrest + _log_sigmoid(z))
            kk_refs[d][rows, :] = one_minus_lb * _sigmoid(-z)
            cum = _split_dot(pair_bf[d], lf)
            cum_refs[d][rows, :] = cum
            min_tot = jnp.minimum(min_tot, jnp.min(chunk_total(cum, d), axis=0))
        return min_tot

    min_tot = lax.fori_loop(0, n_tiles, gates, jnp.zeros((1, HP * K), f32))
    fast_ok = jnp.min(min_tot) > HGRN_FAST_MIN_TOTAL
    for st in st_refs:
        st[...] = jnp.zeros(st.shape, f32)

    def chunk_of(k, d):
        if d == 0:
            return k
        return jnp.where(k < n_ctx_chunks, n_ctx_chunks - 1 - k, n_chunks + n_ctx_chunks - 1 - k)

    def intra(j, carry):
        rows = pl.ds(pl.multiple_of(j * TILE, TILE), TILE)
        for d in (0, 1):
            cum2 = cum_refs[d][rows, :]
            tot4 = chunk_total(cum2, d)
            et_refs[d][pl.ds(pl.multiple_of(j * CPT, CPT), CPT), :, :] = jnp.broadcast_to(
                jnp.exp(tot4), (CPT, 8, HP * K))
            r2 = jnp.broadcast_to(0.5 * tot4, (CPT, CHUNK, HP * K)).reshape(TILE, HP * K)
            e1 = jnp.exp(cum2 - r2)
            e2 = jnp.exp(r2 - cum2)
            er = jnp.exp(r2)
            qa = qv[rows, :] * e1
            kb = kk_refs[d][rows, :] * e2
            qs_refs[d][rows, :] = (qa * er).astype(bf16)
            kd_refs[d][rows, :] = (kb * er).astype(bf16)
            qa = qa.astype(bf16)
            kb = kb.astype(bf16)
            for h in range(HP):
                lanes = slice(K * h, K * (h + 1))
                a = jnp.where(pair_ok[d], _dot_nt(qa[:, lanes], kb[:, lanes]), 0.0)
                y_refs[d][rows, lanes] = _dot(a.astype(bf16), i_ref[rows, lanes].astype(bf16))
        return carry

    def carry_state(k, carry):
        for d in (0, 1):
            c = chunk_of(k, d)
            rows = pl.ds(pl.multiple_of(c * CHUNK, CHUNK), CHUNK)
            et = et_refs[d][c]
            for h in range(HP):
                lanes = slice(K * h, K * (h + 1))
                st = st_refs[2 * h + d]
                s_old = st[...]
                y_refs[d][rows, lanes] += _dot_nt(qs_refs[d][rows, lanes], s_old.astype(bf16))
                st[...] = s_old * et[0:1, lanes] + _dot_tn(i_ref[rows, lanes].astype(bf16), kd_refs[d][rows, lanes])
        return carry

    @pl.when(fast_ok)
    def _():
        lax.fori_loop(0, n_tiles, intra, 0)
        lax.fori_loop(0, n_chunks, carry_state, 0, unroll=HGRN_UNROLL)

    tril = {d: _tri(CHUNK, d == 1) for d in (0, 1)}
    rows_k = lax.broadcasted_iota(jnp.int32, (CHUNK, K), 0)
    cols_c = lax.broadcasted_iota(jnp.int32, (CHUNK, CHUNK), 1)

    def exact_scores(q, kk, cum):
        def col(s, acc):
            sel = rows_k == s
            cs = jnp.sum(jnp.where(sel, cum, 0.0), axis=0, keepdims=True)
            ks = jnp.sum(jnp.where(sel, kk, 0.0), axis=0, keepdims=True)
            w = jnp.exp(jnp.minimum(cum - cs, 0.0))
            return jnp.where(cols_c == s, jnp.sum(q * ks * w, axis=1, keepdims=True), acc)
        return lax.fori_loop(0, CHUNK, col, jnp.zeros((CHUNK, CHUNK), f32))

    def exact_step(k, carry):
        for d in (0, 1):
            c = chunk_of(k, d)
            rows = pl.ds(pl.multiple_of(c * CHUNK, CHUNK), CHUNK)
            for h in range(HP):
                lanes = slice(K * h, K * (h + 1))
                cum = cum_refs[d][rows, lanes]
                kk = kk_refs[d][rows, lanes]
                q = qv[rows, lanes]
                v = i_ref[rows, lanes].astype(bf16)
                tot = cum[0:1, :] if d == 1 else cum[CHUNK - 1:CHUNK, :]
                a = jnp.where(tril[d], exact_scores(q, kk, cum), 0.0)
                st = st_refs[2 * h + d]
                s_old = st[...]
                y_refs[d][rows, lanes] = _dot(a.astype(bf16), v) + _dot_nt(
                    (q * jnp.exp(cum)).astype(bf16), s_old.astype(bf16))
                st[...] = s_old * jnp.exp(tot) + _dot_tn(v, (kk * jnp.exp(tot - cum)).astype(bf16))
        return carry

    @pl.when(jnp.logical_not(fast_ok))
    def _():
        lax.fori_loop(0, n_chunks, exact_step, 0)

    gain = gain_ref[...]

    def readout(c, carry):
        rows = pl.ds(pl.multiple_of(c * ROW_TILE, ROW_TILE), ROW_TILE)
        o = yf[rows, :] + yb[rows, :]
        gate = _silu(g_ref[rows, :])
        y_ref[rows, :] = (jnp.concatenate(
            [_rms(o[:, K * h:K * (h + 1)], gain[:, K * h:K * (h + 1)]) for h in range(HP)], axis=1) * gate
        ).astype(y_ref.dtype)
        return carry

    lax.fori_loop(0, T // ROW_TILE, readout, 0)


def _hgrn(proj, log_lb, log_rest, gain, n_ctx):
    B, T, _ = proj.shape
    wblk = 2 * HGRN_HEAD_DIM
    col = lambda base: pl.BlockSpec((None, T, wblk), lambda b, p: (b, 0, base + p))
    par = pl.BlockSpec((1, wblk), lambda b, p: (0, p))
    K = HGRN_HEAD_DIM
    return pl.pallas_call(
        functools.partial(_hgrn_kernel, n_ctx=n_ctx),
        grid=(B, HGRN_HEADS // 2),
        in_specs=[col(b0) for b0 in HGRN_BLK] + [par, par, par],
        out_specs=pl.BlockSpec((None, T, wblk), lambda b, p: (b, 0, p)),
        out_shape=jax.ShapeDtypeStruct((B, T, BRANCH_WIDTH), bf16),
        scratch_shapes=([pltpu.VMEM((T, wblk), f32)] * 7 + [pltpu.VMEM((T, wblk), bf16)] * 4
                        + [pltpu.VMEM((T // CHUNK, 8, wblk), f32)] * 2 + [pltpu.VMEM((K, K), f32)] * 4),
        compiler_params=_cparams(2), name="hgrn2",
    )(proj, proj, proj, proj, proj, log_lb, log_rest, gain)


def _ssd_kernel(z_ref, x_ref, b_ref, c_ref, dt_ref, cwx_ref, cbx_ref, cwb_ref, cbb_ref, cwc_ref, cbc_ref,
                dtbias_ref, arow_ref, skip_ref, gain_ref, y_ref,
                xstage, bstage, cstage, xs, bs, cs, xd0, xd1, cum0, cum1, dec0, dec1, yf, yb, st0, st1, *, n_ctx):
    T = x_ref.shape[0]
    P = SSD_HEAD_DIM
    R = SSD_HPG
    XW = R * P
    TILE = SCAN_CHUNK
    n_chunks = T // CHUNK
    n_ctx_chunks = n_ctx // CHUNK
    y_refs = (yf, yb)
    st_refs = (st0, st1)
    xd_refs, cum_refs, dec_refs = (xd0, xd1), (cum0, cum1), (dec0, dec1)

    rt = lax.broadcasted_iota(jnp.int32, (TILE, TILE), 0)
    ct = lax.broadcasted_iota(jnp.int32, (TILE, TILE), 1)
    same_chunk = (rt // CHUNK) == (ct // CHUNK)
    pair_bf = {0: (same_chunk & (ct <= rt)).astype(bf16), 1: (same_chunk & (ct >= rt)).astype(bf16)}
    r4 = lax.broadcasted_iota(jnp.int32, (TILE, XW), 0) % CHUNK
    c4 = lax.broadcasted_iota(jnp.int32, (TILE, XW), 1) % CHUNK
    incl4 = {0: c4 <= r4, 1: c4 >= r4}
    strict4 = {0: r4 > c4, 1: r4 < c4}

    _stage_segments(xstage, x_ref, 0, XW, n_ctx, T)
    _stage_segments(bstage, b_ref, 0, SSD_STATE, n_ctx, T)
    _stage_segments(cstage, c_ref, 0, SSD_STATE, n_ctx, T)
    cwx, cbx = cwx_ref[...], cbx_ref[...]
    cwb, cbb = cwb_ref[...], cbb_ref[...]
    cwc, cbc = cwc_ref[...], cbc_ref[...]
    dtbias, arow = dtbias_ref[...], arow_ref[...]
    src = lax.broadcasted_iota(jnp.int32, (128, 2 * XW), 0)
    dst = lax.broadcasted_iota(jnp.int32, (128, 2 * XW), 1)
    spread = (src == R * (dst // XW) + (dst % XW) // P).astype(bf16)

    def prologue(j, carry):
        off = _staged_window(j)
        rows = pl.ds(pl.multiple_of(j * SCAN_CHUNK, SCAN_CHUNK), SCAN_CHUNK)
        x = _silu(_conv_chunk(xstage, off, cwx, cbx))
        xs[rows, :] = x
        bs[rows, :] = _silu(_conv_chunk(bstage, off, cwb, cbb)).astype(bf16)
        cs[rows, :] = _silu(_conv_chunk(cstage, off, cwc, cbc)).astype(bf16)
        dt_all = _split_dot_right(_softplus(dt_ref[rows, :] + dtbias), spread)
        for d in (0, 1):
            dt = dt_all[:, XW * d:XW * (d + 1)]
            da = dt * arow[:, XW * d:XW * (d + 1)]
            xd_refs[d][rows, :] = x * dt
            cum_refs[d][rows, :] = _split_dot(pair_bf[d], da)
            logdec = _split_dot(pair_bf[d], jnp.where(strict4[d], da, 0.0))
            dec_refs[d][rows, :] = jnp.where(incl4[d], jnp.exp(logdec), 0.0)
        return carry

    lax.fori_loop(0, T // SCAN_CHUNK, prologue, 0)
    for st in st_refs:
        st[...] = jnp.zeros(st.shape, f32)

    def chunk_of(k, d):
        if d == 0:
            return k
        return jnp.where(k < n_ctx_chunks, n_ctx_chunks - 1 - k, n_chunks + n_ctx_chunks - 1 - k)

    same_head = (lax.broadcasted_iota(jnp.int32, (XW, XW), 0) // P
                 == lax.broadcasted_iota(jnp.int32, (XW, XW), 1) // P)

    def step(k, d):
        c = chunk_of(k, d)
        rows = pl.ds(pl.multiple_of(c * CHUNK, CHUNK), CHUNK)
        cm = cs[rows, :]
        bm = bs[rows, :]
        xdt = xd_refs[d][rows, :]
        cum = cum_refs[d][rows, :]
        tot = cum[0:1, :] if d == 1 else cum[CHUNK - 1:CHUNK, :]
        cb = _dot_nt(cm, jnp.concatenate([bm] * R, axis=0))
        x_heads = jnp.where(same_head, jnp.concatenate([xdt] * R, axis=0), 0.0)
        intra = _dot((cb * dec_refs[d][rows, :]).astype(bf16), x_heads.astype(bf16))
        st = st_refs[d]
        s_old = st[...]
        inter = _dot(cm, s_old.astype(bf16)) * jnp.exp(cum)
        y_refs[d][rows, :] = intra + inter
        st[...] = s_old * jnp.exp(tot) + _dot_tn(bm, (xdt * jnp.exp(tot - cum)).astype(bf16))

    def body(k, carry):
        step(k, 0)
        step(k, 1)
        return carry

    lax.fori_loop(0, n_chunks, body, 0, unroll=SSD_UNROLL)

    skip, gain = skip_ref[...], gain_ref[...]

    def readout(c, carry):
        rows = pl.ds(pl.multiple_of(c * ROW_TILE, ROW_TILE), ROW_TILE)
        y = skip * xs[rows, :] + yf[rows, :] + yb[rows, :]
        y_ref[rows, :] = _rms(y * _silu(z_ref[rows, :]), gain).astype(y_ref.dtype)
        return carry

    lax.fori_loop(0, T // ROW_TILE, readout, 0)


def _ssd(proj, cwx, cbx, cwb, cbb, cwc, cbc, dtbias, arow, skip, gain, n_ctx):
    B, T, _ = proj.shape
    XW = SSD_HPG * SSD_HEAD_DIM
    N = SSD_STATE
    col = lambda base, w: pl.BlockSpec((None, T, w), lambda b, g: (b, 0, base + g))
    par = lambda r, w: pl.BlockSpec((None, r, w), lambda b, g: (g, 0, 0))
    return pl.pallas_call(
        functools.partial(_ssd_kernel, n_ctx=n_ctx),
        grid=(B, SSD_GROUPS),
        in_specs=[col(SSD_Z_BLK, XW), col(SSD_X_BLK, XW), col(SSD_B_BLK, N), col(SSD_C_BLK, N), col(SSD_DT_BLK, 128),
                  par(CONV_WIDTH, XW), par(1, XW), par(CONV_WIDTH, N), par(1, N), par(CONV_WIDTH, N), par(1, N),
                  par(1, 128), par(1, 2 * XW), par(1, XW), par(1, XW)],
        out_specs=pl.BlockSpec((None, T, XW), lambda b, g: (b, 0, g)),
        out_shape=jax.ShapeDtypeStruct((B, T, SSD_WIDTH), bf16),
        scratch_shapes=[pltpu.VMEM((T + 3 * SEG_PAD, XW), f32), pltpu.VMEM((T + 3 * SEG_PAD, N), f32),
                        pltpu.VMEM((T + 3 * SEG_PAD, N), f32),
                        pltpu.VMEM((T, XW), f32), pltpu.VMEM((T, N), bf16), pltpu.VMEM((T, N), bf16),
                        pltpu.VMEM((T, XW), f32), pltpu.VMEM((T, XW), f32), pltpu.VMEM((T, XW), f32),
                        pltpu.VMEM((T, XW), f32), pltpu.VMEM((T, XW), f32), pltpu.VMEM((T, XW), f32),
                        pltpu.VMEM((T, XW), f32), pltpu.VMEM((T, XW), f32),
                        pltpu.VMEM((N, XW), f32), pltpu.VMEM((N, XW), f32)],
        compiler_params=_cparams(2), name="ssd",
    )(proj, proj, proj, proj, proj, cwx, cbx, cwb, cbb, cwc, cbc, dtbias, arow, skip, gain)


def _post_kernel(*refs, emit_next, split_residual):
    if split_residual:
        ctx_ref, lat_ref, *refs = refs
        residual = jnp.where(pl.program_id(1) == 0, ctx_ref[...], lat_ref[...])
    else:
        x_ref, *refs = refs
        residual = x_ref[...]
    gl_ref, ya_ref, yb_ref, yc_ref, yd_ref, m_ref, wbr_ref, wout_ref, wup_ref, wdn_ref, g1_ref, g2_ref, g3_ref, *rest = refs
    D = D_MODEL
    ys = (ya_ref, yb_ref, yc_ref, yd_ref)
    merged = None
    for i in range(N_BRANCH):
        t = _sigmoid(gl_ref[:, D * i:D * (i + 1)]) * _dot(ys[i][...], wbr_ref[i])
        merged = t if merged is None else merged + t
    mix = _dot(merged.astype(bf16), wout_ref[...])
    mod = lambda k: m_ref[:, D * k:D * (k + 1)]
    x1 = residual + mod(2) * _rms(mix, g1_ref[...])
    h2 = (_rms(x1, g2_ref[...]) * (1.0 + mod(4)) + mod(3)).astype(bf16)
    down = None
    hc = 1024
    for c in range(MLP_HIDDEN // hc):
        u = jnp.maximum(_dot(h2, wup_ref[:, hc * c:hc * (c + 1)]), 0.0)
        t = _dot((u * u).astype(bf16), wdn_ref[hc * c:hc * (c + 1), :])
        down = t if down is None else down + t
    x2 = x1 + mod(5) * _rms(down, g3_ref[...])
    if emit_next:
        mn_ref, gn_ref, o_ref, h_ref = rest
        h_ref[...] = _pre_norm(x2, mn_ref, gn_ref)
    else:
        (o_ref,) = rest
    o_ref[...] = x2


def _post(xs, proj, ya, yb, yc, yd, mods_l, wbr, wout, wup, wdn, g1, g2, g3, nxt):
    split = isinstance(xs, tuple)
    B, T, D = proj.shape[0], proj.shape[1], D_MODEL
    W = BRANCH_WIDTH
    nt = T // ROW_TILE
    first = 1 if nxt is None else 0
    assert not (split and first)
    tile = lambda w: pl.BlockSpec((None, ROW_TILE, w), lambda b, i: (b, i + first, 0))
    once = lambda *s: pl.BlockSpec(s, lambda b, i: (0,) * len(s), pipeline_mode=pl.Buffered(1))
    mod_idx = _mod_row_index(B)
    mod_row = pl.BlockSpec((None, 1, 6 * D), lambda b, i: mod_idx(b, i + first))
    out_tile = pl.BlockSpec((None, ROW_TILE, D), lambda b, i: (b, i, 0))
    if split:
        res_specs = [pl.BlockSpec((None, ROW_TILE, D), lambda b, i: (b, 0, 0)),
                     pl.BlockSpec((None, ROW_TILE, D), lambda b, i: (b, jnp.maximum(i - 1, 0), 0))]
        res_args = list(xs)
    else:
        res_specs, res_args = [tile(D)], [xs]
    in_specs = res_specs + [tile(MERGE_COLS), tile(W), tile(W), tile(W), tile(W), mod_row,
                            once(N_BRANCH, W, D), once(D, D), once(D, MLP_HIDDEN), once(MLP_HIDDEN, D),
                            once(1, D), once(1, D), once(1, D)]
    args = res_args + [proj, ya, yb, yc, yd, mods_l, wbr, wout, wup, wdn, g1, g2, g3]
    out_specs, out_shape = out_tile, jax.ShapeDtypeStruct((B, T - first * ROW_TILE, D), f32)
    if nxt is not None:
        in_specs += [mod_row, once(1, D)]
        args += list(nxt)
        out_specs = [out_tile, out_tile]
        out_shape = [out_shape, jax.ShapeDtypeStruct((B, T, D), bf16)]
    return pl.pallas_call(
        functools.partial(_post_kernel, emit_next=nxt is not None, split_residual=split),
        grid=(B, nt - first), in_specs=in_specs, out_specs=out_specs, out_shape=out_shape,
        compiler_params=_cparams(2), name="merge_mlp",
    )(*args)


def _block_diag(w):
    eye = jnp.eye(LRU_BLOCKS, dtype=w.dtype)
    return jnp.einsum('hij,hg->higj', w, eye).reshape(LRU_WIDTH, LRU_WIDTH)


def _rope_tables(n_lat):
    half = ATTN_HEAD_DIM // 2
    quarter = half // 2
    inv_freq = ROPE_BASE ** (-jnp.arange(quarter, dtype=f32) / quarter)
    t = jnp.arange(n_lat, dtype=jnp.int32)
    rows = (t // GRID_W).astype(f32)[:, None] * inv_freq
    cols = (t % GRID_W).astype(f32)[:, None] * inv_freq
    cos_h = jnp.concatenate([jnp.cos(rows), jnp.cos(rows), jnp.cos(cols), jnp.cos(cols)], axis=-1)
    sin_h = jnp.concatenate([-jnp.sin(rows), jnp.sin(rows), -jnp.sin(cols), jnp.sin(cols)], axis=-1)
    return jnp.tile(cos_h, (1, 2)), jnp.tile(sin_h, (1, 2))


def kernel(x, c, ctx, c_ctx, w_ada, b_ada, g_pre_mix, g_post_mix, g_pre_mlp, g_post_mlp, w_in, lru_conv_w, lru_conv_b, lru_rec_w, lru_rec_b, lru_inp_w, lru_inp_b, lru_lambda, attn_sink, hgrn_lb_logits, hgrn_norm_g, ssd_conv_w, ssd_conv_b, ssd_dt_bias, ssd_a_log, ssd_skip, ssd_norm_g, w_branch, w_out, w_mlp_up, w_mlp_down):
    B, L, D = x.shape
    n_ctx = ctx.shape[1]
    assert D == D_MODEL and n_ctx == ROW_TILE == SCAN_CHUNK and L % ROW_TILE == 0 and L % GRID_W == 0
    assert w_in.shape == (DEPTH, D_MODEL, IN_COLS)

    pad_rows = (-(B + 1)) % 8
    c_all = jnp.concatenate([c, c_ctx[None, :], jnp.zeros((pad_rows, D), f32)], axis=0)
    mods = _ada(c_all, w_ada, b_ada)
    mods = mods.reshape(DEPTH, c_all.shape[0], 1, 6 * D)

    lb = jnp.cumsum(jax.nn.softmax(hgrn_lb_logits.astype(f32), axis=0), axis=0)
    lb = lb - lb[0]
    cos_t, sin_t = _rope_tables(L)
    G, R, P, N = SSD_GROUPS, SSD_HPG, SSD_HEAD_DIM, SSD_STATE

    for l in range(DEPTH):
        last = l == DEPTH - 1
        wl = w_in[l].astype(bf16)
        dtf, dtb = wl[:, 5888:5896], wl[:, 5896:5904]
        zpad = jnp.zeros((D, 128 - 2 * R), bf16)
        dt_cols = [jnp.concatenate([dtf[:, R * g:R * (g + 1)], dtb[:, R * g:R * (g + 1)], zpad], axis=1)
                   for g in range(G)]
        w_perm = jnp.concatenate([wl[:, 5904:], wl[:, :5888]] + dt_cols, axis=1)

        if l == 0:
            xs = (ctx, x)
            proj = _first_in_proj(ctx, x, mods[0], g_pre_mix[0][None, :], w_perm)
        else:
            proj = _in_proj(h, w_perm)

        wbd = jnp.concatenate([_block_diag(lru_rec_w[l, 0]), _block_diag(lru_inp_w[l, 0]),
                               _block_diag(lru_rec_w[l, 1]), _block_diag(lru_inp_w[l, 1])], axis=1).astype(bf16)
        gb = jnp.concatenate([lru_rec_b[l, 0], lru_inp_b[l, 0], lru_rec_b[l, 1], lru_inp_b[l, 1]])[None, :]
        csp = LRU_C * jax.nn.softplus(-lru_lambda[l])
        ya = _lru(proj, lru_conv_w[l], lru_conv_b[l][None, :], wbd, gb, csp, n_ctx)

        yb = _attn(proj, attn_sink[l], cos_t, sin_t, n_ctx)

        lbl = lb[l][None, :]
        yc = _hgrn(proj, jnp.log(lbl), jnp.log1p(-lbl), hgrn_norm_g[l][None, :], n_ctx)

        cw, cbv = ssd_conv_w[l], ssd_conv_b[l]
        XW = R * P
        grp = lambda a, off, w: jnp.stack([a[..., off + w * g:off + w * (g + 1)] for g in range(G)], axis=0)
        cwx, cbx = grp(cw, 0, XW), grp(cbv[None, :], 0, XW)
        cwb, cbb = grp(cw, SSD_WIDTH, N), grp(cbv[None, :], SSD_WIDTH, N)
        cwc, cbc = grp(cw, SSD_WIDTH + G * N, N), grp(cbv[None, :], SSD_WIDTH + G * N, N)
        a_neg = -jnp.exp(ssd_a_log[l].astype(f32))
        per_group = lambda v: [jnp.concatenate([v[0, R * g:R * (g + 1)], v[1, R * g:R * (g + 1)]]) for g in range(G)]
        lanes_compact = lambda v: jnp.stack([jnp.pad(u, (0, 128 - 2 * R))[None, :] for u in per_group(v)], axis=0)
        lanes_spread = lambda v: jnp.stack([jnp.repeat(u, P)[None, :] for u in per_group(v)], axis=0)
        skip = jnp.repeat(ssd_skip[l], P).reshape(G, 1, XW)
        gain_d = ssd_norm_g[l].reshape(G, 1, XW)
        yd = _ssd(proj, cwx, cbx, cwb, cbb, cwc, cbc, lanes_compact(ssd_dt_bias[l]), lanes_spread(a_neg), skip, gain_d, n_ctx)

        nxt = None if last else (mods[l + 1], g_pre_mix[l + 1][None, :])
        out = _post(xs, proj, ya, yb, yc, yd, mods[l], w_branch[l].astype(bf16), w_out[l].astype(bf16),
                    w_mlp_up[l].astype(bf16), w_mlp_down[l].astype(bf16),
                    g_post_mix[l][None, :], g_pre_mlp[l][None, :], g_post_mlp[l][None, :], nxt)
        if last:
            return out
        xs, h = out
```

```python
import functools

import numpy as np
import jax
import jax.numpy as jnp
from jax import lax
from jax.experimental import pallas as pl
from jax.experimental.pallas import tpu as pltpu

f32 = jnp.float32
bf16 = jnp.bfloat16

D_MODEL = 1024
DEPTH = 2
GRID_W = 64
N_BRANCH = 4
BRANCH_WIDTH = D_MODEL // 2
CONV_WIDTH = 4
LRU_WIDTH = BRANCH_WIDTH
LRU_BLOCKS = 8
LRU_BLOCK_W = LRU_WIDTH // LRU_BLOCKS
LRU_C = 8.0
ATTN_HEAD_DIM = 64
ATTN_HEADS = BRANCH_WIDTH // ATTN_HEAD_DIM
ATTN_KV_HEADS = 2
ATTN_REP = ATTN_HEADS // ATTN_KV_HEADS
ATTN_WINDOW = 128
ATTN_BLOCK = 128
ROPE_BASE = 10000.0
HGRN_HEAD_DIM = 128
HGRN_HEADS = BRANCH_WIDTH // HGRN_HEAD_DIM
SSD_WIDTH = BRANCH_WIDTH
SSD_HEAD_DIM = 64
SSD_HEADS = SSD_WIDTH // SSD_HEAD_DIM
SSD_GROUPS = 2
SSD_HPG = SSD_HEADS // SSD_GROUPS
SSD_STATE = 128
MLP_HIDDEN = 4 * D_MODEL
NORM_EPS = 1e-6

IN_COLS = 10000
MERGE_COLS = N_BRANCH * D_MODEL
PROJ_COLS = 10240
LRU_BLK = 4
ATTN_Q_BLK = 10
ATTN_KV_BLK = 22
HGRN_BLK = (23, 25, 27, 29, 31)
SSD_Z_BLK = 33
SSD_X_BLK = 35
SSD_B_BLK = 74
SSD_C_BLK = 76
SSD_DT_BLK = 78

ROW_TILE = 256
SCAN_CHUNK = 256
CHUNK = 64
SEG_PAD = 8
LRU_U_PITCH = 56
LRU_G_PITCH = 40
HGRN_FAST_MIN_TOTAL = -160.0
HGRN_UNROLL = 12
SSD_UNROLL = 12
MASK_BIAS = -1e30
VMEM_LIMIT = 58 * 1024 * 1024


def _cparams(n_axes):
    return pltpu.CompilerParams(dimension_semantics=("arbitrary",) * n_axes, vmem_limit_bytes=VMEM_LIMIT)


def _sigmoid(x):
    return 0.5 * jnp.tanh(0.5 * x) + 0.5


def _silu(x):
    return x * jax.nn.sigmoid(x)


def _gelu_tanh(x):
    return 0.5 * x * (1.0 + jnp.tanh(np.float32(np.sqrt(2.0 / np.pi)) * (x + 0.044715 * (x * x * x))))


def _expm1_given_exp(x, ex):
    poly = x * (1.0 + x * (1.0 / 2) * (1.0 + x * (1.0 / 3) * (1.0 + x * (1.0 / 4))))
    return jnp.where(jnp.abs(x) < 0.03, poly, ex - 1.0)


def _sqrt_nonneg(y):
    return jnp.where(y > 0.0, y * lax.rsqrt(y), 0.0)


def _log_sigmoid(x):
    return jnp.minimum(x, 0.0) - jnp.log(1.0 + jnp.exp(-jnp.abs(x)))


def _logaddexp(a, b):
    return jnp.maximum(a, b) + jnp.log(1.0 + jnp.exp(-jnp.abs(a - b)))


def _softplus(x):
    return jnp.maximum(x, 0.0) + jnp.log1p(jnp.exp(-jnp.abs(x)))


def _rms(x, g):
    return x * lax.rsqrt(jnp.mean(x * x, axis=-1, keepdims=True) + NORM_EPS) * g


def _dot(a, b):
    return jnp.dot(a, b, preferred_element_type=f32)


def _dot_nt(a, b):
    return lax.dot_general(a, b, (((1,), (1,)), ((), ())), preferred_element_type=f32)


def _dot_tn(a, b):
    return lax.dot_general(a, b, (((0,), (0,)), ((), ())), preferred_element_type=f32)


def _split_dot(tri, x):
    hi = x.astype(bf16)
    lo = (x - hi.astype(f32)).astype(bf16)
    return _dot(tri, hi) + _dot(tri, lo)


def _split_dot_right(x, sel):
    hi = x.astype(bf16)
    lo = (x - hi.astype(f32)).astype(bf16)
    return _dot(hi, sel) + _dot(lo, sel)


def _ada_kernel(c_ref, w_ref, b_ref, o_ref):
    cond = _silu(c_ref[...])
    o_ref[...] = jnp.dot(cond, w_ref[...], preferred_element_type=f32,
                         precision=lax.Precision.HIGHEST) + b_ref[...]


def _ada(c_all, w_ada, b_ada):
    rows = c_all.shape[0]
    nt = 6 * D_MODEL // 1024
    return pl.pallas_call(
        _ada_kernel,
        grid=(DEPTH, nt),
        in_specs=[pl.BlockSpec((rows, D_MODEL), lambda l, j: (0, 0)),
                  pl.BlockSpec((None, D_MODEL, 1024), lambda l, j: (l, 0, j)),
                  pl.BlockSpec((None, 1, 1024), lambda l, j: (l, 0, j))],
        out_specs=pl.BlockSpec((None, rows, 1024), lambda l, j: (l, 0, j)),
        out_shape=jax.ShapeDtypeStruct((DEPTH, rows, 6 * D_MODEL), f32),
        compiler_params=_cparams(2), name="ada_mod",
    )(c_all, w_ada, b_ada.reshape(DEPTH, 1, 6 * D_MODEL))


def _mod_row_index(n_batch):
    return lambda b, i: (jnp.where(i == 0, n_batch, b), 0, 0)


def _pre_norm(x, m_ref, g_ref):
    shift = m_ref[:, 0:D_MODEL]
    scale = m_ref[:, D_MODEL:2 * D_MODEL]
    return (_rms(x, g_ref[...]) * (1.0 + scale) + shift).astype(bf16)


def _proj_kernel(a_ref, w_ref, o_ref):
    o_ref[...] = _dot(a_ref[...], w_ref[...])


def _first_proj_kernel(ctx_ref, x_ref, mc_ref, ml_ref, g_ref, w_ref, o_ref, h_ref):
    n_ctx = ctx_ref.shape[0]

    @pl.when(pl.program_id(1) == 0)
    def _():
        h_ref[0:n_ctx, :] = _pre_norm(ctx_ref[...], mc_ref, g_ref)
        h_ref[n_ctx:, :] = _pre_norm(x_ref[...], ml_ref, g_ref)

    o_ref[...] = _dot(h_ref[...], w_ref[...])


def _first_in_proj(ctx, x, mods_l, g, w_perm):
    B, L, D = x.shape
    n_ctx = ctx.shape[1]
    T = n_ctx + L
    tn = 1024
    return pl.pallas_call(
        _first_proj_kernel,
        grid=(B, PROJ_COLS // tn),
        in_specs=[pl.BlockSpec((None, n_ctx, D), lambda b, j: (b, 0, 0)),
                  pl.BlockSpec((None, L, D), lambda b, j: (b, 0, 0)),
                  pl.BlockSpec((None, 1, 6 * D), lambda b, j: (B, 0, 0)),
                  pl.BlockSpec((None, 1, 6 * D), lambda b, j: (b, 0, 0)),
                  pl.BlockSpec((1, D), lambda b, j: (0, 0)),
                  pl.BlockSpec((D, tn), lambda b, j: (0, j))],
        out_specs=pl.BlockSpec((None, T, tn), lambda b, j: (b, 0, j)),
        out_shape=jax.ShapeDtypeStruct((B, T, PROJ_COLS), f32),
        scratch_shapes=[pltpu.VMEM((T, D), bf16)],
        compiler_params=_cparams(2), name="first_in_proj",
    )(ctx, x, mods_l, mods_l, g, w_perm)


def _in_proj(h, w_perm):
    B, T, _ = h.shape
    tn = 1024
    return pl.pallas_call(
        _proj_kernel,
        grid=(B, PROJ_COLS // tn),
        in_specs=[pl.BlockSpec((None, T, D_MODEL), lambda b, j: (b, 0, 0)),
                  pl.BlockSpec((D_MODEL, tn), lambda b, j: (0, j))],
        out_specs=pl.BlockSpec((None, T, tn), lambda b, j: (b, 0, j)),
        out_shape=jax.ShapeDtypeStruct((B, T, PROJ_COLS), f32),
        compiler_params=_cparams(2), name="in_proj",
    )(h, w_perm)


def _stage_segments(dst, src_ref, col0, width, n_ctx, t_all):
    z = jnp.zeros((SEG_PAD, width), f32)
    dst[0:SEG_PAD, :] = z
    dst[SEG_PAD:SEG_PAD + n_ctx, :] = src_ref[0:n_ctx, col0:col0 + width]
    dst[SEG_PAD + n_ctx:2 * SEG_PAD + n_ctx, :] = z
    dst[2 * SEG_PAD + n_ctx:2 * SEG_PAD + t_all, :] = src_ref[n_ctx:t_all, col0:col0 + width]
    dst[2 * SEG_PAD + t_all:3 * SEG_PAD + t_all, :] = z


def _staged_window(j):
    return pl.multiple_of(SCAN_CHUNK * j + jnp.where(j >= 1, SEG_PAD, 0), SEG_PAD)


def _conv_chunk(stage_ref, win0, cw, cb):
    n = SCAN_CHUNK + 2 * SEG_PAD
    win = stage_ref[pl.ds(win0, n), :]
    acc = cb + cw[2:3, :] * win[SEG_PAD:SEG_PAD + SCAN_CHUNK]
    for tap, o in ((0, -2), (1, -1), (3, 1)):
        acc = acc + cw[tap:tap + 1, :] * pltpu.roll(win, (-o) % n, 0)[SEG_PAD:SEG_PAD + SCAN_CHUNK]
    return acc


def _lru_kernel(p_ref, cw_ref, cb_ref, wbd_ref, gb_ref, csp_ref, y_ref, stage, useg, gseg, hfwd, *, n_ctx):
    T = p_ref.shape[0]
    W = LRU_WIDTH
    SEG = SCAN_CHUNK // 8
    n_tiles = W // 128
    n_chunks = T // SCAN_CHUNK
    for c in range(n_tiles):
        _stage_segments(stage.at[c], p_ref, 128 * c, 128, n_ctx, T)
        for j in range(n_chunks):
            win0 = SCAN_CHUNK * j + (SEG_PAD if j >= 1 else 0)
            for s in range(8):
                slot = 8 * j + s
                useg[c, LRU_U_PITCH * slot:LRU_U_PITCH * slot + SEG + 2 * SEG_PAD, :] = stage[
                    c, win0 + SEG * s:win0 + SEG * (s + 1) + 2 * SEG_PAD, :]
                gseg[c, LRU_G_PITCH * slot:LRU_G_PITCH * slot + SEG, :] = p_ref[
                    SCAN_CHUNK * j + SEG * s:SCAN_CHUNK * j + SEG * (s + 1), W + 128 * c:W + 128 * (c + 1)]
    cw = cw_ref[...]
    cb = cb_ref[...]

    def segment_rows(ref, row0, pitch):
        return jnp.concatenate([ref[c, pl.ds(row0, 8, stride=pitch), :] for c in range(n_tiles)], axis=1)

    def chunk(j, carry, d):
        reverse = d == 1
        u0 = pl.multiple_of(j * (8 * LRU_U_PITCH), 8) + SEG_PAD
        v = {k: segment_rows(useg, u0 + k, LRU_U_PITCH) for k in range(-2, SEG + 1)}
        x = jnp.concatenate(
            [cb + cw[0:1, :] * v[k - 2] + cw[1:2, :] * v[k - 1] + cw[2:3, :] * v[k] + cw[3:4, :] * v[k + 1]
             for k in range(SEG)], axis=0)
        g = _dot(x.astype(bf16), wbd_ref[:, 2 * W * d:2 * W * (d + 1)]) + gb_ref[:, 2 * W * d:2 * W * (d + 1)]
        r = _sigmoid(g[:, 0:W])
        i = _sigmoid(g[:, W:2 * W])
        log_a = -csp_ref[d:d + 1, :] * r
        a = jnp.exp(log_a)
        b = _sqrt_nonneg(-_expm1_given_exp(2.0 * log_a, a * a)) * (i * x)
        order = range(SEG - 1, -1, -1) if reverse else range(SEG)
        hl, pk = [None] * SEG, [None] * SEG
        h = p = None
        for k in order:
            ak, bk = a[8 * k:8 * k + 8], b[8 * k:8 * k + 8]
            h = bk if h is None else ak * h + bk
            p = ak if p is None else ak * p
            hl[k], pk[k] = h, p
        seg_in = [None] * 8
        for s in (range(7, -1, -1) if reverse else range(8)):
            seg_in[s] = carry
            carry = h[s:s + 1, :] + p[s:s + 1, :] * carry
        seg_in = jnp.concatenate(seg_in, axis=0)
        return [hl[k] + pk[k] * seg_in for k in range(SEG)], carry

    zero = jnp.zeros((1, W), f32)

    def fwd_body(j, carry):
        hs, carry = chunk(j, carry, 0)
        base = pl.multiple_of(j * SCAN_CHUNK, SCAN_CHUNK)
        for k in range(SEG):
            hfwd[pl.ds(base + 8 * k, 8), :] = hs[k]
        return carry

    lax.fori_loop(0, n_chunks, fwd_body, zero)

    def bwd_body(n, carry):
        j = jnp.where(n == 0, 0, n_chunks - n)
        hs, carry = chunk(j, carry, 1)
        base = pl.multiple_of(j * SCAN_CHUNK, SCAN_CHUNK)
        g0 = pl.multiple_of(j * (8 * LRU_G_PITCH), 8)
        for k in range(SEG):
            yk = (hfwd[pl.ds(base + 8 * k, 8), :] + hs[k]) * _gelu_tanh(segment_rows(gseg, g0 + k, LRU_G_PITCH))
            for c in range(n_tiles):
                gseg[c, pl.ds(g0 + k, 8, stride=LRU_G_PITCH), :] = yk[:, 128 * c:128 * (c + 1)]
        for c in range(n_tiles):
            for s in range(8):
                y_ref[pl.ds(base + SEG * s, SEG), 128 * c:128 * (c + 1)] = gseg[
                    c, pl.ds(g0 + LRU_G_PITCH * s, SEG), :].astype(y_ref.dtype)
        return carry

    lax.fori_loop(0, n_chunks, bwd_body, zero)


def _lru(proj, cw, cb, wbd, gb, csp, n_ctx):
    B, T, _ = proj.shape
    W = LRU_WIDTH
    full = lambda *s: pl.BlockSpec(s, lambda b: (0,) * len(s))
    return pl.pallas_call(
        functools.partial(_lru_kernel, n_ctx=n_ctx),
        grid=(B,),
        in_specs=[pl.BlockSpec((None, T, 2 * W), lambda b: (b, 0, LRU_BLK)),
                  full(CONV_WIDTH, W), full(1, W), full(W, 4 * W), full(1, 4 * W), full(2, W)],
        out_specs=pl.BlockSpec((None, T, W), lambda b: (b, 0, 0)),
        out_shape=jax.ShapeDtypeStruct((B, T, W), bf16),
        scratch_shapes=[pltpu.VMEM((W // 128, T + 3 * SEG_PAD, 128), f32),
                        pltpu.VMEM((W // 128, T // SCAN_CHUNK * 8 * LRU_U_PITCH, 128), f32),
                        pltpu.VMEM((W // 128, T // SCAN_CHUNK * 8 * LRU_G_PITCH, 128), f32),
                        pltpu.VMEM((T, W), f32)],
        compiler_params=_cparams(1), name="rglru",
    )(proj, cw, cb, wbd, gb, csp)


def _attn_kernel(sink_ref, q_ref, kv_ref, cos_ref, sin_ref, y_ref, qs, kp, vp, kc, vc, *, n_ctx):
    T = q_ref.shape[0]
    L = T - n_ctx
    hd = ATTN_HEAD_DIM
    G, R = ATTN_KV_HEADS, ATTN_REP
    nb = L // ATTN_BLOCK
    span = 3 * ATTN_BLOCK
    scale = np.float32(hd ** -0.5)

    def swap_halves(x):
        lane = lax.broadcasted_iota(jnp.int32, x.shape, 1)
        return jnp.where((lane & 31) < 16, pltpu.roll(x, 128 - 16, 1), pltpu.roll(x, 16, 1))

    zpad = jnp.zeros((ATTN_BLOCK, hd), bf16)
    for g in range(G):
        kp[g, 0:ATTN_BLOCK, :] = zpad
        kp[g, ATTN_BLOCK + L:2 * ATTN_BLOCK + L, :] = zpad
        vp[g, 0:ATTN_BLOCK, :] = zpad
        vp[g, ATTN_BLOCK + L:2 * ATTN_BLOCK + L, :] = zpad
        kc[g] = kv_ref[0:n_ctx, hd * g:hd * g + hd].astype(bf16)
        vc[g] = kv_ref[0:n_ctx, hd * (G + g):hd * (G + g) + hd].astype(bf16)

    def rope_rows(c, carry):
        r0 = pl.multiple_of(c * ATTN_BLOCK, ATTN_BLOCK)
        cos = cos_ref[pl.ds(r0, ATTN_BLOCK), :]
        sin = sin_ref[pl.ds(r0, ATTN_BLOCK), :]
        for s in range(ATTN_HEADS * hd // 128):
            xq = q_ref[pl.ds(n_ctx + r0, ATTN_BLOCK), 128 * s:128 * s + 128]
            xr = ((xq * cos + swap_halves(xq) * sin) * scale).astype(bf16)
            qs[2 * s, pl.ds(r0, ATTN_BLOCK), :] = xr[:, 0:hd]
            qs[2 * s + 1, pl.ds(r0, ATTN_BLOCK), :] = xr[:, hd:2 * hd]
        xk = kv_ref[pl.ds(n_ctx + r0, ATTN_BLOCK), 0:128]
        xr = (xk * cos + swap_halves(xk) * sin).astype(bf16)
        xv = kv_ref[pl.ds(n_ctx + r0, ATTN_BLOCK), 128:256].astype(bf16)
        for g in range(G):
            kp[g, pl.ds(ATTN_BLOCK + r0, ATTN_BLOCK), :] = xr[:, hd * g:hd * g + hd]
            vp[g, pl.ds(ATTN_BLOCK + r0, ATTN_BLOCK), :] = xv[:, hd * g:hd * g + hd]
        return carry

    lax.fori_loop(0, nb, rope_rows, 0)

    def sink_column(g, rows_per_head):
        rows = lax.broadcasted_iota(jnp.int32, (R * rows_per_head, 1), 0)
        col = jnp.full((R * rows_per_head, 1), sink_ref[R * g + R - 1], f32)
        for r in range(R - 2, -1, -1):
            col = jnp.where(rows < (r + 1) * rows_per_head, sink_ref[R * g + r], col)
        return col

    def softmax_pv(pieces, sink, v):
        blocks = [s[:, 128 * j:128 * (j + 1)] for s in pieces for j in range(s.shape[1] // 128)]
        m = functools.reduce(jnp.maximum, blocks)
        m = jnp.maximum(jnp.max(m, axis=-1, keepdims=True), sink)
        ps = [jnp.exp(s - m) for s in blocks]
        den = jnp.sum(functools.reduce(jnp.add, ps), axis=-1, keepdims=True) + jnp.exp(sink - m)
        return _dot(jnp.concatenate([p.astype(bf16) for p in ps], axis=1), v) * (1.0 / den)

    qi = lax.broadcasted_iota(jnp.int32, (R * ATTN_BLOCK, ATTN_BLOCK), 0) & (ATTN_BLOCK - 1)
    kj = lax.broadcasted_iota(jnp.int32, (R * ATTN_BLOCK, ATTN_BLOCK), 1)
    bias_a = jnp.where(kj - qi >= ATTN_BLOCK - ATTN_WINDOW, 0.0, MASK_BIAS).astype(f32)
    bias_c = jnp.where(kj - qi <= ATTN_WINDOW - ATTN_BLOCK, 0.0, MASK_BIAS).astype(f32)

    def block(n, carry):
        r0 = pl.multiple_of(n * ATTN_BLOCK, ATTN_BLOCK)
        ba = bias_a + jnp.where(n == 0, MASK_BIAS, 0.0)
        bc = bias_c + jnp.where(n == nb - 1, MASK_BIAS, 0.0)
        for g in range(G):
            q = jnp.concatenate([qs[R * g + r, pl.ds(r0, ATTN_BLOCK), :] for r in range(R)], axis=0)
            k = jnp.concatenate([kp[g, pl.ds(r0, span), :], kc[g]], axis=0)
            v = jnp.concatenate([vp[g, pl.ds(r0, span), :], vc[g]], axis=0)
            s = _dot_nt(q, k)
            pieces = [s[:, 0:ATTN_BLOCK] + ba, s[:, ATTN_BLOCK:2 * ATTN_BLOCK],
                      s[:, 2 * ATTN_BLOCK:span] + bc, s[:, span:]]
            o = softmax_pv(pieces, sink_column(g, ATTN_BLOCK), v)
            y_ref[pl.ds(n_ctx + r0, ATTN_BLOCK), R * hd * g:R * hd * (g + 1)] = jnp.concatenate(
                [o[ATTN_BLOCK * r:ATTN_BLOCK * (r + 1)] for r in range(R)], axis=1).astype(y_ref.dtype)
        return carry

    lax.fori_loop(0, nb, block, 0, unroll=2)

    for g in range(G):
        q = jnp.concatenate(
            [(q_ref[0:n_ctx, hd * (R * g + r):hd * (R * g + r + 1)] * scale).astype(bf16) for r in range(R)], axis=0)
        o = softmax_pv([_dot_nt(q, kc[g])], sink_column(g, n_ctx), vc[g])
        y_ref[0:n_ctx, R * hd * g:R * hd * (g + 1)] = jnp.concatenate(
            [o[n_ctx * r:n_ctx * (r + 1)] for r in range(R)], axis=1).astype(y_ref.dtype)


def _attn(proj, sink, cos_t, sin_t, n_ctx):
    B, T, _ = proj.shape
    L = T - n_ctx
    W = BRANCH_WIDTH
    hd = ATTN_HEAD_DIM
    return pl.pallas_call(
        functools.partial(_attn_kernel, n_ctx=n_ctx),
        grid=(B,),
        in_specs=[pl.BlockSpec(memory_space=pltpu.SMEM),
                  pl.BlockSpec((None, T, W), lambda b: (b, 0, ATTN_Q_BLK)),
                  pl.BlockSpec((None, T, 256), lambda b: (b, 0, ATTN_KV_BLK)),
                  pl.BlockSpec((L, 128), lambda b: (0, 0)),
                  pl.BlockSpec((L, 128), lambda b: (0, 0))],
        out_specs=pl.BlockSpec((None, T, W), lambda b: (b, 0, 0)),
        out_shape=jax.ShapeDtypeStruct((B, T, W), bf16),
        scratch_shapes=[pltpu.VMEM((ATTN_HEADS, L, hd), bf16),
                        pltpu.VMEM((ATTN_KV_HEADS, L + 2 * ATTN_BLOCK, hd), bf16),
                        pltpu.VMEM((ATTN_KV_HEADS, L + 2 * ATTN_BLOCK, hd), bf16),
                        pltpu.VMEM((ATTN_KV_HEADS, n_ctx, hd), bf16),
                        pltpu.VMEM((ATTN_KV_HEADS, n_ctx, hd), bf16)],
        compiler_params=_cparams(1), name="window_attn",
    )(sink, proj, proj, cos_t, sin_t)


def _hgrn_kernel(q_ref, i_ref, ff_ref, fb_ref, g_ref, llb_ref, lrest_ref, gain_ref, y_ref,
                 yf, yb, qs0, qs1, kd0, kd1, et0, et1, st0, st1, st2, st3, *, n_ctx, zero_lb):
    T = q_ref.shape[0]
    K = HGRN_HEAD_DIM
    HP = 2
    TILE = ROW_TILE
    CPT = TILE // CHUNK
    n_tiles = T // TILE
    n_chunks = T // CHUNK
    n_ctx_chunks = n_ctx // CHUNK
    z_refs, y_refs = (ff_ref, fb_ref), (yf, yb)
    qs_refs, kd_refs, et_refs = (qs0, qs1), (kd0, kd1), (et0, et1)
    st_refs = (st0, st1, st2, st3)

    def chunk_total(cum, d):
        c4 = cum.reshape(CPT, CHUNK, cum.shape[-1])
        return c4[:, 0:1, :] if d == 1 else c4[:, CHUNK - 1:CHUNK, :]

    rt = lax.broadcasted_iota(jnp.int32, (TILE, TILE), 0)
    ct = lax.broadcasted_iota(jnp.int32, (TILE, TILE), 1)
    same_chunk = (rt // CHUNK) == (ct // CHUNK)
    pair_ok = {0: same_chunk & (ct <= rt), 1: same_chunk & (ct >= rt)}
    pair_bf = {d: pair_ok[d].astype(bf16) for d in (0, 1)}

    llb = llb_ref[...]
    lrest = lrest_ref[...]
    one_minus_lb = jnp.exp(lrest)

    def gate_terms(z):
        if zero_lb:
            return _log_sigmoid(z), _sigmoid(-z)
        return _logaddexp(llb, lrest + _log_sigmoid(z)), one_minus_lb * _sigmoid(-z)

    def tile_pass(j, min_tot):
        rows = pl.ds(pl.multiple_of(j * TILE, TILE), TILE)
        q = _silu(q_ref[rows, :])
        for d in (0, 1):
            lf, kk = gate_terms(z_refs[d][rows, :])
            cum2 = _split_dot(pair_bf[d], lf)
            tot4 = chunk_total(cum2, d)
            min_tot = jnp.minimum(min_tot, jnp.min(tot4, axis=0))
            et_refs[d][pl.ds(pl.multiple_of(j * CPT, CPT), CPT), :, :] = jnp.broadcast_to(
                jnp.exp(tot4), (CPT, 8, HP * K))
            r2 = jnp.broadcast_to(0.5 * tot4, (CPT, CHUNK, HP * K)).reshape(TILE, HP * K)
            e1 = jnp.exp(cum2 - r2)
            e2 = jnp.exp(r2 - cum2)
            er = jnp.exp(r2)
            qa = q * e1
            kb = kk * e2
            qs_refs[d][rows, :] = (qa * er).astype(bf16)
            kd_refs[d][rows, :] = (kb * er).astype(bf16)
            qa = qa.astype(bf16)
            kb = kb.astype(bf16)
            for h in range(HP):
                lanes = slice(K * h, K * (h + 1))
                a = jnp.where(pair_ok[d], _dot_nt(qa[:, lanes], kb[:, lanes]), 0.0)
                y_refs[d][rows, lanes] = _dot(a.astype(bf16), i_ref[rows, lanes].astype(bf16))
        return min_tot

    min_tot = lax.fori_loop(0, n_tiles, tile_pass, jnp.zeros((1, HP * K), f32), unroll=3)
    fast_ok = jnp.min(min_tot) > HGRN_FAST_MIN_TOTAL
    for st in st_refs:
        st[...] = jnp.zeros(st.shape, f32)

    def chunk_of(k, d):
        if d == 0:
            return k
        return jnp.where(k < n_ctx_chunks, n_ctx_chunks - 1 - k, n_chunks + n_ctx_chunks - 1 - k)

    def carry_state(k, carry):
        for d in (0, 1):
            c = chunk_of(k, d)
            rows = pl.ds(pl.multiple_of(c * CHUNK, CHUNK), CHUNK)
            et = et_refs[d][c]
            for h in range(HP):
                lanes = slice(K * h, K * (h + 1))
                st = st_refs[2 * h + d]
                s_old = st[...]
                y_refs[d][rows, lanes] += _dot_nt(qs_refs[d][rows, lanes], s_old.astype(bf16))
                st[...] = s_old * et[0:1, lanes] + _dot_tn(i_ref[rows, lanes].astype(bf16), kd_refs[d][rows, lanes])
        return carry

    @pl.when(fast_ok)
    def _():
        lax.fori_loop(0, n_chunks, carry_state, 0, unroll=HGRN_UNROLL)

    tril = {d: pair_ok[d][0:CHUNK, 0:CHUNK] for d in (0, 1)}
    tril_bf = {d: tril[d].astype(bf16) for d in (0, 1)}
    rows_k = lax.broadcasted_iota(jnp.int32, (CHUNK, K), 0)
    cols_c = lax.broadcasted_iota(jnp.int32, (CHUNK, CHUNK), 1)

    def exact_scores(q, kk, cum):
        def col(s, acc):
            sel = rows_k == s
            cs = jnp.sum(jnp.where(sel, cum, 0.0), axis=0, keepdims=True)
            ks = jnp.sum(jnp.where(sel, kk, 0.0), axis=0, keepdims=True)
            w = jnp.exp(jnp.minimum(cum - cs, 0.0))
            return jnp.where(cols_c == s, jnp.sum(q * ks * w, axis=1, keepdims=True), acc)
        return lax.fori_loop(0, CHUNK, col, jnp.zeros((CHUNK, CHUNK), f32))

    def exact_step(k, carry):
        for d in (0, 1):
            c = chunk_of(k, d)
            rows = pl.ds(pl.multiple_of(c * CHUNK, CHUNK), CHUNK)
            q2 = _silu(q_ref[rows, :])
            lf2, kk2 = gate_terms(z_refs[d][rows, :])
            cum2 = _split_dot(tril_bf[d], lf2)
            for h in range(HP):
                lanes = slice(K * h, K * (h + 1))
                cum, kk, q = cum2[:, lanes], kk2[:, lanes], q2[:, lanes]
                v = i_ref[rows, lanes].astype(bf16)
                tot = cum[0:1, :] if d == 1 else cum[CHUNK - 1:CHUNK, :]
                a = jnp.where(tril[d], exact_scores(q, kk, cum), 0.0)
                st = st_refs[2 * h + d]
                s_old = st[...]
                y_refs[d][rows, lanes] = _dot(a.astype(bf16), v) + _dot_nt(
                    (q * jnp.exp(cum)).astype(bf16), s_old.astype(bf16))
                st[...] = s_old * jnp.exp(tot) + _dot_tn(v, (kk * jnp.exp(tot - cum)).astype(bf16))
        return carry

    @pl.when(jnp.logical_not(fast_ok))
    def _():
        lax.fori_loop(0, n_chunks, exact_step, 0)

    gain = gain_ref[...]

    def readout(c, carry):
        rows = pl.ds(pl.multiple_of(c * ROW_TILE, ROW_TILE), ROW_TILE)
        o = yf[rows, :] + yb[rows, :]
        gate = _silu(g_ref[rows, :])
        y_ref[rows, :] = (jnp.concatenate(
            [_rms(o[:, K * h:K * (h + 1)], gain[:, K * h:K * (h + 1)]) for h in range(HP)], axis=1) * gate
        ).astype(y_ref.dtype)
        return carry

    lax.fori_loop(0, T // ROW_TILE, readout, 0)


def _hgrn(proj, log_lb, log_rest, gain, n_ctx, zero_lb):
    B, T, _ = proj.shape
    wblk = 2 * HGRN_HEAD_DIM
    col = lambda base: pl.BlockSpec((None, T, wblk), lambda b, p: (b, 0, base + p))
    par = pl.BlockSpec((1, wblk), lambda b, p: (0, p))
    K = HGRN_HEAD_DIM
    return pl.pallas_call(
        functools.partial(_hgrn_kernel, n_ctx=n_ctx, zero_lb=zero_lb),
        grid=(B, HGRN_HEADS // 2),
        in_specs=[col(b0) for b0 in HGRN_BLK] + [par, par, par],
        out_specs=pl.BlockSpec((None, T, wblk), lambda b, p: (b, 0, p)),
        out_shape=jax.ShapeDtypeStruct((B, T, BRANCH_WIDTH), bf16),
        scratch_shapes=([pltpu.VMEM((T, wblk), f32)] * 2 + [pltpu.VMEM((T, wblk), bf16)] * 4
                        + [pltpu.VMEM((T // CHUNK, 8, wblk), f32)] * 2 + [pltpu.VMEM((K, K), f32)] * 4),
        compiler_params=_cparams(2), name="hgrn2",
    )(proj, proj, proj, proj, proj, log_lb, log_rest, gain)


def _ssd_kernel(z_ref, x_ref, b_ref, c_ref, dt_ref, cwx_ref, cbx_ref, cwb_ref, cbb_ref, cwc_ref, cbc_ref,
                dtbias_ref, arow_ref, skip_ref, gain_ref, y_ref,
                xstage, bstage, cstage, xs, bs, cs, xd0, xd1, cum0, cum1, dec0, dec1, yf, yb, st0, st1, *, n_ctx):
    T = x_ref.shape[0]
    P = SSD_HEAD_DIM
    R = SSD_HPG
    XW = R * P
    TILE = SCAN_CHUNK
    n_chunks = T // CHUNK
    n_ctx_chunks = n_ctx // CHUNK
    y_refs = (yf, yb)
    st_refs = (st0, st1)
    xd_refs, cum_refs, dec_refs = (xd0, xd1), (cum0, cum1), (dec0, dec1)

    rt = lax.broadcasted_iota(jnp.int32, (TILE, TILE), 0)
    ct = lax.broadcasted_iota(jnp.int32, (TILE, TILE), 1)
    same_chunk = (rt // CHUNK) == (ct // CHUNK)
    pair_bf = {0: (same_chunk & (ct <= rt)).astype(bf16), 1: (same_chunk & (ct >= rt)).astype(bf16)}
    r4 = lax.broadcasted_iota(jnp.int32, (TILE, XW), 0) % CHUNK
    c4 = lax.broadcasted_iota(jnp.int32, (TILE, XW), 1) % CHUNK
    incl4 = {0: c4 <= r4, 1: c4 >= r4}
    strict4 = {0: r4 > c4, 1: r4 < c4}

    _stage_segments(xstage, x_ref, 0, XW, n_ctx, T)
    _stage_segments(bstage, b_ref, 0, SSD_STATE, n_ctx, T)
    _stage_segments(cstage, c_ref, 0, SSD_STATE, n_ctx, T)
    cwx, cbx = cwx_ref[...], cbx_ref[...]
    cwb, cbb = cwb_ref[...], cbb_ref[...]
    cwc, cbc = cwc_ref[...], cbc_ref[...]
    dtbias, arow = dtbias_ref[...], arow_ref[...]
    src = lax.broadcasted_iota(jnp.int32, (128, 2 * XW), 0)
    dst = lax.broadcasted_iota(jnp.int32, (128, 2 * XW), 1)
    spread = (src == R * (dst // XW) + (dst % XW) // P).astype(bf16)

    def prologue(j, carry):
        off = _staged_window(j)
        rows = pl.ds(pl.multiple_of(j * SCAN_CHUNK, SCAN_CHUNK), SCAN_CHUNK)
        x = _silu(_conv_chunk(xstage, off, cwx, cbx))
        xs[rows, :] = x
        bs[rows, :] = _silu(_conv_chunk(bstage, off, cwb, cbb)).astype(bf16)
        cs[rows, :] = _silu(_conv_chunk(cstage, off, cwc, cbc)).astype(bf16)
        dt_all = _split_dot_right(_softplus(dt_ref[rows, :] + dtbias), spread)
        for d in (0, 1):
            dt = dt_all[:, XW * d:XW * (d + 1)]
            da = dt * arow[:, XW * d:XW * (d + 1)]
            xd_refs[d][rows, :] = x * dt
            cum_refs[d][rows, :] = _split_dot(pair_bf[d], da)
            logdec = _split_dot(pair_bf[d], jnp.where(strict4[d], da, 0.0))
            dec_refs[d][rows, :] = jnp.where(incl4[d], jnp.exp(logdec), 0.0)
        return carry

    lax.fori_loop(0, T // SCAN_CHUNK, prologue, 0)
    for st in st_refs:
        st[...] = jnp.zeros(st.shape, f32)

    def chunk_of(k, d):
        if d == 0:
            return k
        return jnp.where(k < n_ctx_chunks, n_ctx_chunks - 1 - k, n_chunks + n_ctx_chunks - 1 - k)

    same_head = (lax.broadcasted_iota(jnp.int32, (XW, XW), 0) // P
                 == lax.broadcasted_iota(jnp.int32, (XW, XW), 1) // P)

    def step(k, d):
        c = chunk_of(k, d)
        rows = pl.ds(pl.multiple_of(c * CHUNK, CHUNK), CHUNK)
        cm = cs[rows, :]
        bm = bs[rows, :]
        xdt = xd_refs[d][rows, :]
        cum = cum_refs[d][rows, :]
        tot = cum[0:1, :] if d == 1 else cum[CHUNK - 1:CHUNK, :]
        cb = _dot_nt(cm, jnp.concatenate([bm] * R, axis=0))
        x_heads = jnp.where(same_head, jnp.concatenate([xdt] * R, axis=0), 0.0)
        intra = _dot((cb * dec_refs[d][rows, :]).astype(bf16), x_heads.astype(bf16))
        st = st_refs[d]
        s_old = st[...]
        inter = _dot(cm, s_old.astype(bf16)) * jnp.exp(cum)
        y_refs[d][rows, :] = intra + inter
        st[...] = s_old * jnp.exp(tot) + _dot_tn(bm, (xdt * jnp.exp(tot - cum)).astype(bf16))

    def body(k, carry):
        step(k, 0)
        step(k, 1)
        return carry

    lax.fori_loop(0, n_chunks, body, 0, unroll=SSD_UNROLL)

    skip, gain = skip_ref[...], gain_ref[...]

    def readout(c, carry):
        rows = pl.ds(pl.multiple_of(c * ROW_TILE, ROW_TILE), ROW_TILE)
        y = skip * xs[rows, :] + yf[rows, :] + yb[rows, :]
        y_ref[rows, :] = _rms(y * _silu(z_ref[rows, :]), gain).astype(y_ref.dtype)
        return carry

    lax.fori_loop(0, T // ROW_TILE, readout, 0)


def _ssd(proj, cwx, cbx, cwb, cbb, cwc, cbc, dtbias, arow, skip, gain, n_ctx):
    B, T, _ = proj.shape
    XW = SSD_HPG * SSD_HEAD_DIM
    N = SSD_STATE
    col = lambda base, w: pl.BlockSpec((None, T, w), lambda b, g: (b, 0, base + g))
    par = lambda r, w: pl.BlockSpec((None, r, w), lambda b, g: (g, 0, 0))
    return pl.pallas_call(
        functools.partial(_ssd_kernel, n_ctx=n_ctx),
        grid=(B, SSD_GROUPS),
        in_specs=[col(SSD_Z_BLK, XW), col(SSD_X_BLK, XW), col(SSD_B_BLK, N), col(SSD_C_BLK, N), col(SSD_DT_BLK, 128),
                  par(CONV_WIDTH, XW), par(1, XW), par(CONV_WIDTH, N), par(1, N), par(CONV_WIDTH, N), par(1, N),
                  par(1, 128), par(1, 2 * XW), par(1, XW), par(1, XW)],
        out_specs=pl.BlockSpec((None, T, XW), lambda b, g: (b, 0, g)),
        out_shape=jax.ShapeDtypeStruct((B, T, SSD_WIDTH), bf16),
        scratch_shapes=[pltpu.VMEM((T + 3 * SEG_PAD, XW), f32), pltpu.VMEM((T + 3 * SEG_PAD, N), f32),
                        pltpu.VMEM((T + 3 * SEG_PAD, N), f32),
                        pltpu.VMEM((T, XW), f32), pltpu.VMEM((T, N), bf16), pltpu.VMEM((T, N), bf16),
                        pltpu.VMEM((T, XW), f32), pltpu.VMEM((T, XW), f32), pltpu.VMEM((T, XW), f32),
                        pltpu.VMEM((T, XW), f32), pltpu.VMEM((T, XW), f32), pltpu.VMEM((T, XW), f32),
                        pltpu.VMEM((T, XW), f32), pltpu.VMEM((T, XW), f32),
                        pltpu.VMEM((N, XW), f32), pltpu.VMEM((N, XW), f32)],
        compiler_params=_cparams(2), name="ssd",
    )(proj, proj, proj, proj, proj, cwx, cbx, cwb, cbb, cwc, cbc, dtbias, arow, skip, gain)


def _post_kernel(*refs, emit_next, split_residual):
    if split_residual:
        ctx_ref, lat_ref, *refs = refs
        residual = jnp.where(pl.program_id(1) == 0, ctx_ref[...], lat_ref[...])
    else:
        x_ref, *refs = refs
        residual = x_ref[...]
    gl_ref, ya_ref, yb_ref, yc_ref, yd_ref, m_ref, wbr_ref, wout_ref, wup_ref, wdn_ref, g1_ref, g2_ref, g3_ref, *rest = refs
    D = D_MODEL
    ys = (ya_ref, yb_ref, yc_ref, yd_ref)
    merged = None
    for i in range(N_BRANCH):
        t = _sigmoid(gl_ref[:, D * i:D * (i + 1)]) * _dot(ys[i][...], wbr_ref[i])
        merged = t if merged is None else merged + t
    mix = _dot(merged.astype(bf16), wout_ref[...])
    mod = lambda k: m_ref[:, D * k:D * (k + 1)]
    x1 = residual + mod(2) * _rms(mix, g1_ref[...])
    h2 = (_rms(x1, g2_ref[...]) * (1.0 + mod(4)) + mod(3)).astype(bf16)
    down = None
    hc = 1024
    for c in range(MLP_HIDDEN // hc):
        u = jnp.maximum(_dot(h2, wup_ref[:, hc * c:hc * (c + 1)]), 0.0)
        t = _dot((u * u).astype(bf16), wdn_ref[hc * c:hc * (c + 1), :])
        down = t if down is None else down + t
    x2 = x1 + mod(5) * _rms(down, g3_ref[...])
    if emit_next:
        mn_ref, gn_ref, o_ref, h_ref = rest
        h_ref[...] = _pre_norm(x2, mn_ref, gn_ref)
    else:
        (o_ref,) = rest
    o_ref[...] = x2


def _post(xs, proj, ya, yb, yc, yd, mods_l, wbr, wout, wup, wdn, g1, g2, g3, nxt):
    split = isinstance(xs, tuple)
    B, T, D = proj.shape[0], proj.shape[1], D_MODEL
    W = BRANCH_WIDTH
    nt = T // ROW_TILE
    first = 1 if nxt is None else 0
    assert not (split and first)
    tile = lambda w: pl.BlockSpec((None, ROW_TILE, w), lambda b, i: (b, i + first, 0))
    once = lambda *s: pl.BlockSpec(s, lambda b, i: (0,) * len(s), pipeline_mode=pl.Buffered(1))
    mod_idx = _mod_row_index(B)
    mod_row = pl.BlockSpec((None, 1, 6 * D), lambda b, i: mod_idx(b, i + first))
    out_tile = pl.BlockSpec((None, ROW_TILE, D), lambda b, i: (b, i, 0))
    if split:
        res_specs = [pl.BlockSpec((None, ROW_TILE, D), lambda b, i: (b, 0, 0)),
                     pl.BlockSpec((None, ROW_TILE, D), lambda b, i: (b, jnp.maximum(i - 1, 0), 0))]
        res_args = list(xs)
    else:
        res_specs, res_args = [tile(D)], [xs]
    in_specs = res_specs + [tile(MERGE_COLS), tile(W), tile(W), tile(W), tile(W), mod_row,
                            once(N_BRANCH, W, D), once(D, D), once(D, MLP_HIDDEN), once(MLP_HIDDEN, D),
                            once(1, D), once(1, D), once(1, D)]
    args = res_args + [proj, ya, yb, yc, yd, mods_l, wbr, wout, wup, wdn, g1, g2, g3]
    out_specs, out_shape = out_tile, jax.ShapeDtypeStruct((B, T - first * ROW_TILE, D), f32)
    if nxt is not None:
        in_specs += [mod_row, once(1, D)]
        args += list(nxt)
        out_specs = [out_tile, out_tile]
        out_shape = [out_shape, jax.ShapeDtypeStruct((B, T, D), bf16)]
    return pl.pallas_call(
        functools.partial(_post_kernel, emit_next=nxt is not None, split_residual=split),
        grid=(B, nt - first), in_specs=in_specs, out_specs=out_specs, out_shape=out_shape,
        compiler_params=_cparams(2), name="merge_mlp",
    )(*args)


def _block_diag(w):
    eye = jnp.eye(LRU_BLOCKS, dtype=w.dtype)
    return jnp.einsum('hij,hg->higj', w, eye).reshape(LRU_WIDTH, LRU_WIDTH)


def _rope_tables(n_lat):
    half = ATTN_HEAD_DIM // 2
    quarter = half // 2
    inv_freq = ROPE_BASE ** (-jnp.arange(quarter, dtype=f32) / quarter)
    t = jnp.arange(n_lat, dtype=jnp.int32)
    rows = (t // GRID_W).astype(f32)[:, None] * inv_freq
    cols = (t % GRID_W).astype(f32)[:, None] * inv_freq
    cos_h = jnp.concatenate([jnp.cos(rows), jnp.cos(rows), jnp.cos(cols), jnp.cos(cols)], axis=-1)
    sin_h = jnp.concatenate([-jnp.sin(rows), jnp.sin(rows), -jnp.sin(cols), jnp.sin(cols)], axis=-1)
    return jnp.tile(cos_h, (1, 2)), jnp.tile(sin_h, (1, 2))


def kernel(x, c, ctx, c_ctx, w_ada, b_ada, g_pre_mix, g_post_mix, g_pre_mlp, g_post_mlp, w_in, lru_conv_w, lru_conv_b, lru_rec_w, lru_rec_b, lru_inp_w, lru_inp_b, lru_lambda, attn_sink, hgrn_lb_logits, hgrn_norm_g, ssd_conv_w, ssd_conv_b, ssd_dt_bias, ssd_a_log, ssd_skip, ssd_norm_g, w_branch, w_out, w_mlp_up, w_mlp_down):
    B, L, D = x.shape
    n_ctx = ctx.shape[1]
    assert D == D_MODEL and n_ctx == ROW_TILE == SCAN_CHUNK and L % ROW_TILE == 0 and L % GRID_W == 0
    assert w_in.shape == (DEPTH, D_MODEL, IN_COLS)

    pad_rows = (-(B + 1)) % 8
    c_all = jnp.concatenate([c, c_ctx[None, :], jnp.zeros((pad_rows, D), f32)], axis=0)
    mods = _ada(c_all, w_ada, b_ada)
    mods = mods.reshape(DEPTH, c_all.shape[0], 1, 6 * D)

    lb = jnp.cumsum(jax.nn.softmax(hgrn_lb_logits.astype(f32), axis=0), axis=0)
    lb = lb - lb[0]
    cos_t, sin_t = _rope_tables(L)
    G, R, P, N = SSD_GROUPS, SSD_HPG, SSD_HEAD_DIM, SSD_STATE

    for l in range(DEPTH):
        last = l == DEPTH - 1
        wl = w_in[l].astype(bf16)
        dtf, dtb = wl[:, 5888:5896], wl[:, 5896:5904]
        zpad = jnp.zeros((D, 128 - 2 * R), bf16)
        dt_cols = [jnp.concatenate([dtf[:, R * g:R * (g + 1)], dtb[:, R * g:R * (g + 1)], zpad], axis=1)
                   for g in range(G)]
        w_perm = jnp.concatenate([wl[:, 5904:], wl[:, :5888]] + dt_cols, axis=1)

        if l == 0:
            xs = (ctx, x)
            proj = _first_in_proj(ctx, x, mods[0], g_pre_mix[0][None, :], w_perm)
        else:
            proj = _in_proj(h, w_perm)

        wbd = jnp.concatenate([_block_diag(lru_rec_w[l, 0]), _block_diag(lru_inp_w[l, 0]),
                               _block_diag(lru_rec_w[l, 1]), _block_diag(lru_inp_w[l, 1])], axis=1).astype(bf16)
        gb = jnp.concatenate([lru_rec_b[l, 0], lru_inp_b[l, 0], lru_rec_b[l, 1], lru_inp_b[l, 1]])[None, :]
        csp = LRU_C * jax.nn.softplus(-lru_lambda[l])
        ya = _lru(proj, lru_conv_w[l], lru_conv_b[l][None, :], wbd, gb, csp, n_ctx)

        yb = _attn(proj, attn_sink[l], cos_t, sin_t, n_ctx)

        lbl = lb[l][None, :]
        yc = _hgrn(proj, jnp.log(lbl), jnp.log1p(-lbl), hgrn_norm_g[l][None, :], n_ctx, zero_lb=l == 0)

        cw, cbv = ssd_conv_w[l], ssd_conv_b[l]
        XW = R * P
        grp = lambda a, off, w: jnp.stack([a[..., off + w * g:off + w * (g + 1)] for g in range(G)], axis=0)
        cwx, cbx = grp(cw, 0, XW), grp(cbv[None, :], 0, XW)
        cwb, cbb = grp(cw, SSD_WIDTH, N), grp(cbv[None, :], SSD_WIDTH, N)
        cwc, cbc = grp(cw, SSD_WIDTH + G * N, N), grp(cbv[None, :], SSD_WIDTH + G * N, N)
        a_neg = -jnp.exp(ssd_a_log[l].astype(f32))
        per_group = lambda v: [jnp.concatenate([v[0, R * g:R * (g + 1)], v[1, R * g:R * (g + 1)]]) for g in range(G)]
        lanes_compact = lambda v: jnp.stack([jnp.pad(u, (0, 128 - 2 * R))[None, :] for u in per_group(v)], axis=0)
        lanes_spread = lambda v: jnp.stack([jnp.repeat(u, P)[None, :] for u in per_group(v)], axis=0)
        skip = jnp.repeat(ssd_skip[l], P).reshape(G, 1, XW)
        gain_d = ssd_norm_g[l].reshape(G, 1, XW)
        yd = _ssd(proj, cwx, cbx, cwb, cbb, cwc, cbc, lanes_compact(ssd_dt_bias[l]), lanes_spread(a_neg), skip, gain_d, n_ctx)

        nxt = None if last else (mods[l + 1], g_pre_mix[l + 1][None, :])
        out = _post(xs, proj, ya, yb, yc, yd, mods[l], w_branch[l].astype(bf16), w_out[l].astype(bf16),
                    w_mlp_up[l].astype(bf16), w_mlp_down[l].astype(bf16),
                    g_post_mix[l][None, :], g_pre_mlp[l][None, :], g_post_mlp[l][None, :], nxt)
        if last:
            return out
        xs, h = out
```

```python
import functools

import numpy as np
import jax
import jax.numpy as jnp
from jax import lax
from jax.experimental import pallas as pl
from jax.experimental.pallas import tpu as pltpu

f32 = jnp.float32
bf16 = jnp.bfloat16

D_MODEL = 1024
DEPTH = 2
GRID_W = 64
N_BRANCH = 4
BRANCH_WIDTH = D_MODEL // 2
CONV_WIDTH = 4
LRU_WIDTH = BRANCH_WIDTH
LRU_BLOCKS = 8
LRU_BLOCK_W = LRU_WIDTH // LRU_BLOCKS
LRU_C = 8.0
ATTN_HEAD_DIM = 64
ATTN_HEADS = BRANCH_WIDTH // ATTN_HEAD_DIM
ATTN_KV_HEADS = 2
ATTN_REP = ATTN_HEADS // ATTN_KV_HEADS
ATTN_WINDOW = 128
ATTN_BLOCK = 128
ROPE_BASE = 10000.0
HGRN_HEAD_DIM = 128
HGRN_HEADS = BRANCH_WIDTH // HGRN_HEAD_DIM
SSD_WIDTH = BRANCH_WIDTH
SSD_HEAD_DIM = 64
SSD_HEADS = SSD_WIDTH // SSD_HEAD_DIM
SSD_GROUPS = 2
SSD_HPG = SSD_HEADS // SSD_GROUPS
SSD_STATE = 128
MLP_HIDDEN = 4 * D_MODEL
NORM_EPS = 1e-6

IN_COLS = 10000
MERGE_COLS = N_BRANCH * D_MODEL
PROJ_COLS = 10240
LRU_BLK = 4
ATTN_Q_BLK = 10
ATTN_KV_BLK = 22
HGRN_BLK = (23, 25, 27, 29, 31)
SSD_Z_BLK = 33
SSD_X_BLK = 35
SSD_B_BLK = 74
SSD_C_BLK = 76
SSD_DT_BLK = 78

ROW_TILE = 256
SCAN_CHUNK = 256
CHUNK = 64
SEG_PAD = 8
LRU_U_PITCH = 56
LRU_G_PITCH = 40
HGRN_FAST_MIN_TOTAL = -160.0
HGRN_UNROLL = 12
SSD_UNROLL = 12
MASK_BIAS = -1e30
VMEM_LIMIT = 58 * 1024 * 1024


def _cparams(n_axes):
    return pltpu.CompilerParams(dimension_semantics=("arbitrary",) * n_axes, vmem_limit_bytes=VMEM_LIMIT)


def _sigmoid(x):
    return 0.5 * jnp.tanh(0.5 * x) + 0.5


def _silu(x):
    return x * jax.nn.sigmoid(x)


def _gelu_tanh(x):
    return 0.5 * x * (1.0 + jnp.tanh(np.float32(np.sqrt(2.0 / np.pi)) * (x + 0.044715 * (x * x * x))))


def _expm1_given_exp(x, ex):
    poly = x * (1.0 + x * (1.0 / 2) * (1.0 + x * (1.0 / 3) * (1.0 + x * (1.0 / 4))))
    return jnp.where(jnp.abs(x) < 0.03, poly, ex - 1.0)


def _sqrt_nonneg(y):
    return jnp.where(y > 0.0, y * lax.rsqrt(y), 0.0)


def _log_sigmoid(x):
    return jnp.minimum(x, 0.0) - jnp.log(1.0 + jnp.exp(-jnp.abs(x)))


def _logaddexp(a, b):
    return jnp.maximum(a, b) + jnp.log(1.0 + jnp.exp(-jnp.abs(a - b)))


def _softplus(x):
    return jnp.maximum(x, 0.0) + jnp.log1p(jnp.exp(-jnp.abs(x)))


def _rms(x, g):
    return x * lax.rsqrt(jnp.mean(x * x, axis=-1, keepdims=True) + NORM_EPS) * g


def _dot(a, b):
    return jnp.dot(a, b, preferred_element_type=f32)


def _dot_nt(a, b):
    return lax.dot_general(a, b, (((1,), (1,)), ((), ())), preferred_element_type=f32)


def _dot_tn(a, b):
    return lax.dot_general(a, b, (((0,), (0,)), ((), ())), preferred_element_type=f32)


def _split_dot(tri, x):
    hi = x.astype(bf16)
    lo = (x - hi.astype(f32)).astype(bf16)
    return _dot(tri, hi) + _dot(tri, lo)


def _split_dot_right(x, sel):
    hi = x.astype(bf16)
    lo = (x - hi.astype(f32)).astype(bf16)
    return _dot(hi, sel) + _dot(lo, sel)


def _ada_kernel(c_ref, w_ref, b_ref, o_ref):
    cond = _silu(c_ref[...])
    o_ref[...] = jnp.dot(cond, w_ref[...], preferred_element_type=f32,
                         precision=lax.Precision.HIGHEST) + b_ref[...]


def _ada(c_all, w_ada, b_ada):
    rows = c_all.shape[0]
    nt = 6 * D_MODEL // 1024
    return pl.pallas_call(
        _ada_kernel,
        grid=(DEPTH, nt),
        in_specs=[pl.BlockSpec((rows, D_MODEL), lambda l, j: (0, 0)),
                  pl.BlockSpec((None, D_MODEL, 1024), lambda l, j: (l, 0, j)),
                  pl.BlockSpec((None, 1, 1024), lambda l, j: (l, 0, j))],
        out_specs=pl.BlockSpec((None, rows, 1024), lambda l, j: (l, 0, j)),
        out_shape=jax.ShapeDtypeStruct((DEPTH, rows, 6 * D_MODEL), f32),
        compiler_params=_cparams(2), name="ada_mod",
    )(c_all, w_ada, b_ada.reshape(DEPTH, 1, 6 * D_MODEL))


def _mod_row_index(n_batch):
    return lambda b, i: (jnp.where(i == 0, n_batch, b), 0, 0)


def _pre_norm(x, m_ref, g_ref):
    shift = m_ref[:, 0:D_MODEL]
    scale = m_ref[:, D_MODEL:2 * D_MODEL]
    return (_rms(x, g_ref[...]) * (1.0 + scale) + shift).astype(bf16)


def _proj_kernel(a_ref, w_ref, o_ref):
    o_ref[...] = _dot(a_ref[...], w_ref[...])


def _first_proj_kernel(ctx_ref, x_ref, mc_ref, ml_ref, g_ref, w_ref, o_ref, h_ref):
    n_ctx = ctx_ref.shape[0]

    @pl.when(pl.program_id(1) == 0)
    def _():
        h_ref[0:n_ctx, :] = _pre_norm(ctx_ref[...], mc_ref, g_ref)
        h_ref[n_ctx:, :] = _pre_norm(x_ref[...], ml_ref, g_ref)

    o_ref[...] = _dot(h_ref[...], w_ref[...])


def _first_in_proj(ctx, x, mods_l, g, w_perm):
    B, L, D = x.shape
    n_ctx = ctx.shape[1]
    T = n_ctx + L
    tn = 1024
    return pl.pallas_call(
        _first_proj_kernel,
        grid=(B, PROJ_COLS // tn),
        in_specs=[pl.BlockSpec((None, n_ctx, D), lambda b, j: (b, 0, 0)),
                  pl.BlockSpec((None, L, D), lambda b, j: (b, 0, 0)),
                  pl.BlockSpec((None, 1, 6 * D), lambda b, j: (B, 0, 0)),
                  pl.BlockSpec((None, 1, 6 * D), lambda b, j: (b, 0, 0)),
                  pl.BlockSpec((1, D), lambda b, j: (0, 0)),
                  pl.BlockSpec((D, tn), lambda b, j: (0, j))],
        out_specs=pl.BlockSpec((None, T, tn), lambda b, j: (b, 0, j)),
        out_shape=jax.ShapeDtypeStruct((B, T, PROJ_COLS), f32),
        scratch_shapes=[pltpu.VMEM((T, D), bf16)],
        compiler_params=_cparams(2), name="first_in_proj",
    )(ctx, x, mods_l, mods_l, g, w_perm)


def _in_proj(h, w_perm):
    B, T, _ = h.shape
    tn = 1024
    return pl.pallas_call(
        _proj_kernel,
        grid=(B, PROJ_COLS // tn),
        in_specs=[pl.BlockSpec((None, T, D_MODEL), lambda b, j: (b, 0, 0)),
                  pl.BlockSpec((D_MODEL, tn), lambda b, j: (0, j))],
        out_specs=pl.BlockSpec((None, T, tn), lambda b, j: (b, 0, j)),
        out_shape=jax.ShapeDtypeStruct((B, T, PROJ_COLS), f32),
        compiler_params=_cparams(2), name="in_proj",
    )(h, w_perm)


def _stage_segments(dst, src_ref, col0, width, n_ctx, t_all):
    z = jnp.zeros((SEG_PAD, width), f32)
    dst[0:SEG_PAD, :] = z
    dst[SEG_PAD:SEG_PAD + n_ctx, :] = src_ref[0:n_ctx, col0:col0 + width]
    dst[SEG_PAD + n_ctx:2 * SEG_PAD + n_ctx, :] = z
    dst[2 * SEG_PAD + n_ctx:2 * SEG_PAD + t_all, :] = src_ref[n_ctx:t_all, col0:col0 + width]
    dst[2 * SEG_PAD + t_all:3 * SEG_PAD + t_all, :] = z


def _staged_window(j):
    return pl.multiple_of(SCAN_CHUNK * j + jnp.where(j >= 1, SEG_PAD, 0), SEG_PAD)


def _conv_chunk(stage_ref, win0, cw, cb):
    n = SCAN_CHUNK + 2 * SEG_PAD
    win = stage_ref[pl.ds(win0, n), :]
    acc = cb + cw[2:3, :] * win[SEG_PAD:SEG_PAD + SCAN_CHUNK]
    for tap, o in ((0, -2), (1, -1), (3, 1)):
        acc = acc + cw[tap:tap + 1, :] * pltpu.roll(win, (-o) % n, 0)[SEG_PAD:SEG_PAD + SCAN_CHUNK]
    return acc


def _lru_kernel(p_ref, cw_ref, cb_ref, wbd_ref, gb_ref, csp_ref, y_ref, stage, useg, gseg, hfwd, *, n_ctx):
    T = p_ref.shape[0]
    W = LRU_WIDTH
    SEG = SCAN_CHUNK // 8
    n_tiles = W // 128
    n_chunks = T // SCAN_CHUNK
    for c in range(n_tiles):
        _stage_segments(stage.at[c], p_ref, 128 * c, 128, n_ctx, T)
        for j in range(n_chunks):
            win0 = SCAN_CHUNK * j + (SEG_PAD if j >= 1 else 0)
            for s in range(8):
                slot = 8 * j + s
                useg[c, LRU_U_PITCH * slot:LRU_U_PITCH * slot + SEG + 2 * SEG_PAD, :] = stage[
                    c, win0 + SEG * s:win0 + SEG * (s + 1) + 2 * SEG_PAD, :]
                gseg[c, LRU_G_PITCH * slot:LRU_G_PITCH * slot + SEG, :] = p_ref[
                    SCAN_CHUNK * j + SEG * s:SCAN_CHUNK * j + SEG * (s + 1), W + 128 * c:W + 128 * (c + 1)]
    cw = cw_ref[...]
    cb = cb_ref[...]

    def segment_rows(ref, row0, pitch):
        return jnp.concatenate([ref[c, pl.ds(row0, 8, stride=pitch), :] for c in range(n_tiles)], axis=1)

    def chunk(j, carry, d):
        reverse = d == 1
        u0 = pl.multiple_of(j * (8 * LRU_U_PITCH), 8) + SEG_PAD
        v = {k: segment_rows(useg, u0 + k, LRU_U_PITCH) for k in range(-2, SEG + 1)}
        x = jnp.concatenate(
            [cb + cw[0:1, :] * v[k - 2] + cw[1:2, :] * v[k - 1] + cw[2:3, :] * v[k] + cw[3:4, :] * v[k + 1]
             for k in range(SEG)], axis=0)
        g = _dot(x.astype(bf16), wbd_ref[:, 2 * W * d:2 * W * (d + 1)]) + gb_ref[:, 2 * W * d:2 * W * (d + 1)]
        r = _sigmoid(g[:, 0:W])
        i = _sigmoid(g[:, W:2 * W])
        log_a = -csp_ref[d:d + 1, :] * r
        a = jnp.exp(log_a)
        b = _sqrt_nonneg(-_expm1_given_exp(2.0 * log_a, a * a)) * (i * x)
        order = range(SEG - 1, -1, -1) if reverse else range(SEG)
        hl, pk = [None] * SEG, [None] * SEG
        h = p = None
        for k in order:
            ak, bk = a[8 * k:8 * k + 8], b[8 * k:8 * k + 8]
            h = bk if h is None else ak * h + bk
            p = ak if p is None else ak * p
            hl[k], pk[k] = h, p
        seg_in = [None] * 8
        for s in (range(7, -1, -1) if reverse else range(8)):
            seg_in[s] = carry
            carry = h[s:s + 1, :] + p[s:s + 1, :] * carry
        seg_in = jnp.concatenate(seg_in, axis=0)
        return [hl[k] + pk[k] * seg_in for k in range(SEG)], carry

    zero = jnp.zeros((1, W), f32)

    def fwd_body(j, carry):
        hs, carry = chunk(j, carry, 0)
        base = pl.multiple_of(j * SCAN_CHUNK, SCAN_CHUNK)
        for k in range(SEG):
            hfwd[pl.ds(base + 8 * k, 8), :] = hs[k]
        return carry

    lax.fori_loop(0, n_chunks, fwd_body, zero)

    def bwd_body(n, carry):
        j = jnp.where(n == 0, 0, n_chunks - n)
        hs, carry = chunk(j, carry, 1)
        base = pl.multiple_of(j * SCAN_CHUNK, SCAN_CHUNK)
        g0 = pl.multiple_of(j * (8 * LRU_G_PITCH), 8)
        for k in range(SEG):
            yk = (hfwd[pl.ds(base + 8 * k, 8), :] + hs[k]) * _gelu_tanh(segment_rows(gseg, g0 + k, LRU_G_PITCH))
            for c in range(n_tiles):
                gseg[c, pl.ds(g0 + k, 8, stride=LRU_G_PITCH), :] = yk[:, 128 * c:128 * (c + 1)]
        for c in range(n_tiles):
            for s in range(8):
                y_ref[pl.ds(base + SEG * s, SEG), 128 * c:128 * (c + 1)] = gseg[
                    c, pl.ds(g0 + LRU_G_PITCH * s, SEG), :].astype(y_ref.dtype)
        return carry

    lax.fori_loop(0, n_chunks, bwd_body, zero)


def _lru(proj, cw, cb, wbd, gb, csp, n_ctx):
    B, T, _ = proj.shape
    W = LRU_WIDTH
    full = lambda *s: pl.BlockSpec(s, lambda b: (0,) * len(s))
    return pl.pallas_call(
        functools.partial(_lru_kernel, n_ctx=n_ctx),
        grid=(B,),
        in_specs=[pl.BlockSpec((None, T, 2 * W), lambda b: (b, 0, LRU_BLK)),
                  full(CONV_WIDTH, W), full(1, W), full(W, 4 * W), full(1, 4 * W), full(2, W)],
        out_specs=pl.BlockSpec((None, T, W), lambda b: (b, 0, 0)),
        out_shape=jax.ShapeDtypeStruct((B, T, W), bf16),
        scratch_shapes=[pltpu.VMEM((W // 128, T + 3 * SEG_PAD, 128), f32),
                        pltpu.VMEM((W // 128, T // SCAN_CHUNK * 8 * LRU_U_PITCH, 128), f32),
                        pltpu.VMEM((W // 128, T // SCAN_CHUNK * 8 * LRU_G_PITCH, 128), f32),
                        pltpu.VMEM((T, W), f32)],
        compiler_params=_cparams(1), name="rglru",
    )(proj, cw, cb, wbd, gb, csp)


def _attn_kernel(sink_ref, q_ref, kv_ref, cos_ref, sin_ref, y_ref, qs, kp, vp, kc, vc, *, n_ctx):
    T = q_ref.shape[0]
    L = T - n_ctx
    hd = ATTN_HEAD_DIM
    G, R = ATTN_KV_HEADS, ATTN_REP
    nb = L // ATTN_BLOCK
    span = 3 * ATTN_BLOCK
    scale = np.float32(hd ** -0.5)

    def swap_halves(x):
        lane = lax.broadcasted_iota(jnp.int32, x.shape, 1)
        return jnp.where((lane & 31) < 16, pltpu.roll(x, 128 - 16, 1), pltpu.roll(x, 16, 1))

    zpad = jnp.zeros((ATTN_BLOCK, hd), bf16)
    for g in range(G):
        kp[g, 0:ATTN_BLOCK, :] = zpad
        kp[g, ATTN_BLOCK + L:2 * ATTN_BLOCK + L, :] = zpad
        vp[g, 0:ATTN_BLOCK, :] = zpad
        vp[g, ATTN_BLOCK + L:2 * ATTN_BLOCK + L, :] = zpad
        kc[g] = kv_ref[0:n_ctx, hd * g:hd * g + hd].astype(bf16)
        vc[g] = kv_ref[0:n_ctx, hd * (G + g):hd * (G + g) + hd].astype(bf16)

    def rope_rows(c, carry):
        r0 = pl.multiple_of(c * ATTN_BLOCK, ATTN_BLOCK)
        cos = cos_ref[pl.ds(r0, ATTN_BLOCK), :]
        sin = sin_ref[pl.ds(r0, ATTN_BLOCK), :]
        for s in range(ATTN_HEADS * hd // 128):
            xq = q_ref[pl.ds(n_ctx + r0, ATTN_BLOCK), 128 * s:128 * s + 128]
            xr = ((xq * cos + swap_halves(xq) * sin) * scale).astype(bf16)
            qs[2 * s, pl.ds(r0, ATTN_BLOCK), :] = xr[:, 0:hd]
            qs[2 * s + 1, pl.ds(r0, ATTN_BLOCK), :] = xr[:, hd:2 * hd]
        xk = kv_ref[pl.ds(n_ctx + r0, ATTN_BLOCK), 0:128]
        xr = (xk * cos + swap_halves(xk) * sin).astype(bf16)
        xv = kv_ref[pl.ds(n_ctx + r0, ATTN_BLOCK), 128:256].astype(bf16)
        for g in range(G):
            kp[g, pl.ds(ATTN_BLOCK + r0, ATTN_BLOCK), :] = xr[:, hd * g:hd * g + hd]
            vp[g, pl.ds(ATTN_BLOCK + r0, ATTN_BLOCK), :] = xv[:, hd * g:hd * g + hd]
        return carry

    lax.fori_loop(0, nb, rope_rows, 0)

    def sink_column(g, rows_per_head):
        rows = lax.broadcasted_iota(jnp.int32, (R * rows_per_head, 1), 0)
        col = jnp.full((R * rows_per_head, 1), sink_ref[R * g + R - 1], f32)
        for r in range(R - 2, -1, -1):
            col = jnp.where(rows < (r + 1) * rows_per_head, sink_ref[R * g + r], col)
        return col

    def softmax_pv(pieces, sink, v):
        blocks = [s[:, 128 * j:128 * (j + 1)] for s in pieces for j in range(s.shape[1] // 128)]
        m = functools.reduce(jnp.maximum, blocks)
        m = jnp.maximum(jnp.max(m, axis=-1, keepdims=True), sink)
        ps = [jnp.exp(s - m) for s in blocks]
        den = jnp.sum(functools.reduce(jnp.add, ps), axis=-1, keepdims=True) + jnp.exp(sink - m)
        return _dot(jnp.concatenate([p.astype(bf16) for p in ps], axis=1), v) * (1.0 / den)

    qi = lax.broadcasted_iota(jnp.int32, (R * ATTN_BLOCK, ATTN_BLOCK), 0) & (ATTN_BLOCK - 1)
    kj = lax.broadcasted_iota(jnp.int32, (R * ATTN_BLOCK, ATTN_BLOCK), 1)
    bias_a = jnp.where(kj - qi >= ATTN_BLOCK - ATTN_WINDOW, 0.0, MASK_BIAS).astype(f32)
    bias_c = jnp.where(kj - qi <= ATTN_WINDOW - ATTN_BLOCK, 0.0, MASK_BIAS).astype(f32)

    def block(n, carry):
        r0 = pl.multiple_of(n * ATTN_BLOCK, ATTN_BLOCK)
        ba = bias_a + jnp.where(n == 0, MASK_BIAS, 0.0)
        bc = bias_c + jnp.where(n == nb - 1, MASK_BIAS, 0.0)
        for g in range(G):
            q = jnp.concatenate([qs[R * g + r, pl.ds(r0, ATTN_BLOCK), :] for r in range(R)], axis=0)
            k = jnp.concatenate([kp[g, pl.ds(r0, span), :], kc[g]], axis=0)
            v = jnp.concatenate([vp[g, pl.ds(r0, span), :], vc[g]], axis=0)
            s = _dot_nt(q, k)
            pieces = [s[:, 0:ATTN_BLOCK] + ba, s[:, ATTN_BLOCK:2 * ATTN_BLOCK],
                      s[:, 2 * ATTN_BLOCK:span] + bc, s[:, span:]]
            o = softmax_pv(pieces, sink_column(g, ATTN_BLOCK), v)
            y_ref[pl.ds(n_ctx + r0, ATTN_BLOCK), R * hd * g:R * hd * (g + 1)] = jnp.concatenate(
                [o[ATTN_BLOCK * r:ATTN_BLOCK * (r + 1)] for r in range(R)], axis=1).astype(y_ref.dtype)
        return carry

    lax.fori_loop(0, nb, block, 0, unroll=2)

    for g in range(G):
        q = jnp.concatenate(
            [(q_ref[0:n_ctx, hd * (R * g + r):hd * (R * g + r + 1)] * scale).astype(bf16) for r in range(R)], axis=0)
        o = softmax_pv([_dot_nt(q, kc[g])], sink_column(g, n_ctx), vc[g])
        y_ref[0:n_ctx, R * hd * g:R * hd * (g + 1)] = jnp.concatenate(
            [o[n_ctx * r:n_ctx * (r + 1)] for r in range(R)], axis=1).astype(y_ref.dtype)


def _attn(proj, sink, cos_t, sin_t, n_ctx):
    B, T, _ = proj.shape
    L = T - n_ctx
    W = BRANCH_WIDTH
    hd = ATTN_HEAD_DIM
    return pl.pallas_call(
        functools.partial(_attn_kernel, n_ctx=n_ctx),
        grid=(B,),
        in_specs=[pl.BlockSpec(memory_space=pltpu.SMEM),
                  pl.BlockSpec((None, T, W), lambda b: (b, 0, ATTN_Q_BLK)),
                  pl.BlockSpec((None, T, 256), lambda b: (b, 0, ATTN_KV_BLK)),
                  pl.BlockSpec((L, 128), lambda b: (0, 0)),
                  pl.BlockSpec((L, 128), lambda b: (0, 0))],
        out_specs=pl.BlockSpec((None, T, W), lambda b: (b, 0, 0)),
        out_shape=jax.ShapeDtypeStruct((B, T, W), bf16),
        scratch_shapes=[pltpu.VMEM((ATTN_HEADS, L, hd), bf16),
                        pltpu.VMEM((ATTN_KV_HEADS, L + 2 * ATTN_BLOCK, hd), bf16),
                        pltpu.VMEM((ATTN_KV_HEADS, L + 2 * ATTN_BLOCK, hd), bf16),
                        pltpu.VMEM((ATTN_KV_HEADS, n_ctx, hd), bf16),
                        pltpu.VMEM((ATTN_KV_HEADS, n_ctx, hd), bf16)],
        compiler_params=_cparams(1), name="window_attn",
    )(sink, proj, proj, cos_t, sin_t)


def _hgrn_kernel(q_ref, i_ref, ff_ref, fb_ref, g_ref, llb_ref, lrest_ref, gain_ref, y_ref,
                 yf, yb, qs0, qs1, kd0, kd1, et0, et1, st0, st1, st2, st3, *, n_ctx, zero_lb):
    T = q_ref.shape[0]
    K = HGRN_HEAD_DIM
    HP = 2
    TILE = ROW_TILE
    CPT = TILE // CHUNK
    n_tiles = T // TILE
    n_chunks = T // CHUNK
    n_ctx_chunks = n_ctx // CHUNK
    z_refs, y_refs = (ff_ref, fb_ref), (yf, yb)
    qs_refs, kd_refs, et_refs = (qs0, qs1), (kd0, kd1), (et0, et1)
    st_refs = (st0, st1, st2, st3)

    def chunk_total(cum, d):
        c4 = cum.reshape(CPT, CHUNK, cum.shape[-1])
        return c4[:, 0:1, :] if d == 1 else c4[:, CHUNK - 1:CHUNK, :]

    rt = lax.broadcasted_iota(jnp.int32, (TILE, TILE), 0)
    ct = lax.broadcasted_iota(jnp.int32, (TILE, TILE), 1)
    same_chunk = (rt // CHUNK) == (ct // CHUNK)
    pair_ok = {0: same_chunk & (ct <= rt), 1: same_chunk & (ct >= rt)}
    pair_bf = {d: pair_ok[d].astype(bf16) for d in (0, 1)}

    llb = llb_ref[...]
    lrest = lrest_ref[...]
    one_minus_lb = jnp.exp(lrest)

    def gate_terms(z):
        if zero_lb:
            return _log_sigmoid(z), _sigmoid(-z)
        return _logaddexp(llb, lrest + _log_sigmoid(z)), one_minus_lb * _sigmoid(-z)

    def tile_pass(j, min_tot):
        rows = pl.ds(pl.multiple_of(j * TILE, TILE), TILE)
        q = _silu(q_ref[rows, :])
        for d in (0, 1):
            lf, kk = gate_terms(z_refs[d][rows, :])
            cum2 = _split_dot(pair_bf[d], lf)
            tot4 = chunk_total(cum2, d)
            min_tot = jnp.minimum(min_tot, jnp.min(tot4, axis=0))
            et_refs[d][pl.ds(pl.multiple_of(j * CPT, CPT), CPT), :, :] = jnp.broadcast_to(
                jnp.exp(tot4), (CPT, 8, HP * K))
            r2 = jnp.broadcast_to(0.5 * tot4, (CPT, CHUNK, HP * K)).reshape(TILE, HP * K)
            e1 = jnp.exp(cum2 - r2)
            e2 = jnp.exp(r2 - cum2)
            er = jnp.exp(r2)
            qa = q * e1
            kb = kk * e2
            qs_refs[d][rows, :] = (qa * er).astype(bf16)
            kd_refs[d][rows, :] = (kb * er).astype(bf16)
            qa = qa.astype(bf16)
            kb = kb.astype(bf16)
            for h in range(HP):
                lanes = slice(K * h, K * (h + 1))
                a = jnp.where(pair_ok[d], _dot_nt(qa[:, lanes], kb[:, lanes]), 0.0)
                y_refs[d][rows, lanes] = _dot(a.astype(bf16), i_ref[rows, lanes].astype(bf16))
        return min_tot

    min_tot = lax.fori_loop(0, n_tiles, tile_pass, jnp.zeros((1, HP * K), f32), unroll=3)
    fast_ok = jnp.min(min_tot) > HGRN_FAST_MIN_TOTAL
    for st in st_refs:
        st[...] = jnp.zeros(st.shape, f32)

    def chunk_of(k, d):
        if d == 0:
            return k
        return jnp.where(k < n_ctx_chunks, n_ctx_chunks - 1 - k, n_chunks + n_ctx_chunks - 1 - k)

    def carry_state(k, carry):
        for d in (0, 1):
            c = chunk_of(k, d)
            rows = pl.ds(pl.multiple_of(c * CHUNK, CHUNK), CHUNK)
            et = et_refs[d][c]
            for h in range(HP):
                lanes = slice(K * h, K * (h + 1))
                st = st_refs[2 * h + d]
                s_old = st[...]
                y_refs[d][rows, lanes] += _dot_nt(qs_refs[d][rows, lanes], s_old.astype(bf16))
                st[...] = s_old * et[0:1, lanes] + _dot_tn(i_ref[rows, lanes].astype(bf16), kd_refs[d][rows, lanes])
        return carry

    @pl.when(fast_ok)
    def _():
        lax.fori_loop(0, n_chunks, carry_state, 0, unroll=HGRN_UNROLL)

    tril = {d: pair_ok[d][0:CHUNK, 0:CHUNK] for d in (0, 1)}
    tril_bf = {d: tril[d].astype(bf16) for d in (0, 1)}
    rows_k = lax.broadcasted_iota(jnp.int32, (CHUNK, K), 0)
    cols_c = lax.broadcasted_iota(jnp.int32, (CHUNK, CHUNK), 1)

    def exact_scores(q, kk, cum):
        def col(s, acc):
            sel = rows_k == s
            cs = jnp.sum(jnp.where(sel, cum, 0.0), axis=0, keepdims=True)
            ks = jnp.sum(jnp.where(sel, kk, 0.0), axis=0, keepdims=True)
            w = jnp.exp(jnp.minimum(cum - cs, 0.0))
            return jnp.where(cols_c == s, jnp.sum(q * ks * w, axis=1, keepdims=True), acc)
        return lax.fori_loop(0, CHUNK, col, jnp.zeros((CHUNK, CHUNK), f32))

    def exact_step(k, carry):
        for d in (0, 1):
            c = chunk_of(k, d)
            rows = pl.ds(pl.multiple_of(c * CHUNK, CHUNK), CHUNK)
            q2 = _silu(q_ref[rows, :])
            lf2, kk2 = gate_terms(z_refs[d][rows, :])
            cum2 = _split_dot(tril_bf[d], lf2)
            for h in range(HP):
                lanes = slice(K * h, K * (h + 1))
                cum, kk, q = cum2[:, lanes], kk2[:, lanes], q2[:, lanes]
                v = i_ref[rows, lanes].astype(bf16)
                tot = cum[0:1, :] if d == 1 else cum[CHUNK - 1:CHUNK, :]
                a = jnp.where(tril[d], exact_scores(q, kk, cum), 0.0)
                st = st_refs[2 * h + d]
                s_old = st[...]
                y_refs[d][rows, lanes] = _dot(a.astype(bf16), v) + _dot_nt(
                    (q * jnp.exp(cum)).astype(bf16), s_old.astype(bf16))
                st[...] = s_old * jnp.exp(tot) + _dot_tn(v, (kk * jnp.exp(tot - cum)).astype(bf16))
        return carry

    @pl.when(jnp.logical_not(fast_ok))
    def _():
        lax.fori_loop(0, n_chunks, exact_step, 0)

    gain = gain_ref[...]

    def readout(c, carry):
        rows = pl.ds(pl.multiple_of(c * ROW_TILE, ROW_TILE), ROW_TILE)
        o = yf[rows, :] + yb[rows, :]
        gate = _silu(g_ref[rows, :])
        y_ref[rows, :] = (jnp.concatenate(
            [_rms(o[:, K * h:K * (h + 1)], gain[:, K * h:K * (h + 1)]) for h in range(HP)], axis=1) * gate
        ).astype(y_ref.dtype)
        return carry

    lax.fori_loop(0, T // ROW_TILE, readout, 0)


def _hgrn(proj, log_lb, log_rest, gain, n_ctx, zero_lb):
    B, T, _ = proj.shape
    wblk = 2 * HGRN_HEAD_DIM
    col = lambda base: pl.BlockSpec((None, T, wblk), lambda b, p: (b, 0, base + p))
    par = pl.BlockSpec((1, wblk), lambda b, p: (0, p))
    K = HGRN_HEAD_DIM
    return pl.pallas_call(
        functools.partial(_hgrn_kernel, n_ctx=n_ctx, zero_lb=zero_lb),
        grid=(B, HGRN_HEADS // 2),
        in_specs=[col(b0) for b0 in HGRN_BLK] + [par, par, par],
        out_specs=pl.BlockSpec((None, T, wblk), lambda b, p: (b, 0, p)),
        out_shape=jax.ShapeDtypeStruct((B, T, BRANCH_WIDTH), bf16),
        scratch_shapes=([pltpu.VMEM((T, wblk), f32)] * 2 + [pltpu.VMEM((T, wblk), bf16)] * 4
                        + [pltpu.VMEM((T // CHUNK, 8, wblk), f32)] * 2 + [pltpu.VMEM((K, K), f32)] * 4),
        compiler_params=_cparams(2), name="hgrn2",
    )(proj, proj, proj, proj, proj, log_lb, log_rest, gain)


def _ssd_kernel(z_ref, x_ref, b_ref, c_ref, dt_ref, cwx_ref, cbx_ref, cwb_ref, cbb_ref, cwc_ref, cbc_ref,
                dtbias_ref, arow_ref, skip_ref, gain_ref, y_ref,
                xstage, bstage, cstage, xs, bs, cs, xd0, xd1, cum0, cum1, dec0, dec1, yf, yb, st0, st1, *, n_ctx):
    T = x_ref.shape[0]
    P = SSD_HEAD_DIM
    R = SSD_HPG
    XW = R * P
    TILE = SCAN_CHUNK
    n_chunks = T // CHUNK
    n_ctx_chunks = n_ctx // CHUNK
    y_refs = (yf, yb)
    st_refs = (st0, st1)
    xd_refs, cum_refs, dec_refs = (xd0, xd1), (cum0, cum1), (dec0, dec1)

    rt = lax.broadcasted_iota(jnp.int32, (TILE, TILE), 0)
    ct = lax.broadcasted_iota(jnp.int32, (TILE, TILE), 1)
    same_chunk = (rt // CHUNK) == (ct // CHUNK)
    pair_bf = {0: (same_chunk & (ct <= rt)).astype(bf16), 1: (same_chunk & (ct >= rt)).astype(bf16)}
    r4 = lax.broadcasted_iota(jnp.int32, (TILE, XW), 0) % CHUNK
    c4 = lax.broadcasted_iota(jnp.int32, (TILE, XW), 1) % CHUNK
    incl4 = {0: c4 <= r4, 1: c4 >= r4}
    strict4 = {0: r4 > c4, 1: r4 < c4}

    _stage_segments(xstage, x_ref, 0, XW, n_ctx, T)
    _stage_segments(bstage, b_ref, 0, SSD_STATE, n_ctx, T)
    _stage_segments(cstage, c_ref, 0, SSD_STATE, n_ctx, T)
    cwx, cbx = cwx_ref[...], cbx_ref[...]
    cwb, cbb = cwb_ref[...], cbb_ref[...]
    cwc, cbc = cwc_ref[...], cbc_ref[...]
    dtbias, arow = dtbias_ref[...], arow_ref[...]
    src = lax.broadcasted_iota(jnp.int32, (128, 2 * XW), 0)
    dst = lax.broadcasted_iota(jnp.int32, (128, 2 * XW), 1)
    spread = (src == R * (dst // XW) + (dst % XW) // P).astype(bf16)

    def prologue(j, carry):
        off = _staged_window(j)
        rows = pl.ds(pl.multiple_of(j * SCAN_CHUNK, SCAN_CHUNK), SCAN_CHUNK)
        x = _silu(_conv_chunk(xstage, off, cwx, cbx))
        xs[rows, :] = x
        bs[rows, :] = _silu(_conv_chunk(bstage, off, cwb, cbb)).astype(bf16)
        cs[rows, :] = _silu(_conv_chunk(cstage, off, cwc, cbc)).astype(bf16)
        dt_all = _split_dot_right(_softplus(dt_ref[rows, :] + dtbias), spread)
        for d in (0, 1):
            dt = dt_all[:, XW * d:XW * (d + 1)]
            da = dt * arow[:, XW * d:XW * (d + 1)]
            xd_refs[d][rows, :] = x * dt
            cum_refs[d][rows, :] = _split_dot(pair_bf[d], da)
            logdec = _split_dot(pair_bf[d], jnp.where(strict4[d], da, 0.0))
            dec_refs[d][rows, :] = jnp.where(incl4[d], jnp.exp(logdec), 0.0)
        return carry

    lax.fori_loop(0, T // SCAN_CHUNK, prologue, 0, unroll=3)
    for st in st_refs:
        st[...] = jnp.zeros(st.shape, f32)

    def chunk_of(k, d):
        if d == 0:
            return k
        return jnp.where(k < n_ctx_chunks, n_ctx_chunks - 1 - k, n_chunks + n_ctx_chunks - 1 - k)

    same_head = (lax.broadcasted_iota(jnp.int32, (XW, XW), 0) // P
                 == lax.broadcasted_iota(jnp.int32, (XW, XW), 1) // P)

    def step(k, d):
        c = chunk_of(k, d)
        rows = pl.ds(pl.multiple_of(c * CHUNK, CHUNK), CHUNK)
        cm = cs[rows, :]
        bm = bs[rows, :]
        xdt = xd_refs[d][rows, :]
        cum = cum_refs[d][rows, :]
        tot = cum[0:1, :] if d == 1 else cum[CHUNK - 1:CHUNK, :]
        cb = _dot_nt(cm, jnp.concatenate([bm] * R, axis=0))
        x_heads = jnp.where(same_head, jnp.concatenate([xdt] * R, axis=0), 0.0)
        intra = _dot((cb * dec_refs[d][rows, :]).astype(bf16), x_heads.astype(bf16))
        st = st_refs[d]
        s_old = st[...]
        inter = _dot(cm, s_old.astype(bf16)) * jnp.exp(cum)
        y_refs[d][rows, :] = intra + inter
        st[...] = s_old * jnp.exp(tot) + _dot_tn(bm, (xdt * jnp.exp(tot - cum)).astype(bf16))

    def body(k, carry):
        step(k, 0)
        step(k, 1)
        return carry

    lax.fori_loop(0, n_chunks, body, 0, unroll=SSD_UNROLL)

    skip, gain = skip_ref[...], gain_ref[...]

    def readout(c, carry):
        rows = pl.ds(pl.multiple_of(c * ROW_TILE, ROW_TILE), ROW_TILE)
        y = skip * xs[rows, :] + yf[rows, :] + yb[rows, :]
        y_ref[rows, :] = _rms(y * _silu(z_ref[rows, :]), gain).astype(y_ref.dtype)
        return carry

    lax.fori_loop(0, T // ROW_TILE, readout, 0)


def _ssd(proj, cwx, cbx, cwb, cbb, cwc, cbc, dtbias, arow, skip, gain, n_ctx):
    B, T, _ = proj.shape
    XW = SSD_HPG * SSD_HEAD_DIM
    N = SSD_STATE
    col = lambda base, w: pl.BlockSpec((None, T, w), lambda b, g: (b, 0, base + g))
    par = lambda r, w: pl.BlockSpec((None, r, w), lambda b, g: (g, 0, 0))
    return pl.pallas_call(
        functools.partial(_ssd_kernel, n_ctx=n_ctx),
        grid=(B, SSD_GROUPS),
        in_specs=[col(SSD_Z_BLK, XW), col(SSD_X_BLK, XW), col(SSD_B_BLK, N), col(SSD_C_BLK, N), col(SSD_DT_BLK, 128),
                  par(CONV_WIDTH, XW), par(1, XW), par(CONV_WIDTH, N), par(1, N), par(CONV_WIDTH, N), par(1, N),
                  par(1, 128), par(1, 2 * XW), par(1, XW), par(1, XW)],
        out_specs=pl.BlockSpec((None, T, XW), lambda b, g: (b, 0, g)),
        out_shape=jax.ShapeDtypeStruct((B, T, SSD_WIDTH), bf16),
        scratch_shapes=[pltpu.VMEM((T + 3 * SEG_PAD, XW), f32), pltpu.VMEM((T + 3 * SEG_PAD, N), f32),
                        pltpu.VMEM((T + 3 * SEG_PAD, N), f32),
                        pltpu.VMEM((T, XW), f32), pltpu.VMEM((T, N), bf16), pltpu.VMEM((T, N), bf16),
                        pltpu.VMEM((T, XW), f32), pltpu.VMEM((T, XW), f32), pltpu.VMEM((T, XW), f32),
                        pltpu.VMEM((T, XW), f32), pltpu.VMEM((T, XW), f32), pltpu.VMEM((T, XW), f32),
                        pltpu.VMEM((T, XW), f32), pltpu.VMEM((T, XW), f32),
                        pltpu.VMEM((N, XW), f32), pltpu.VMEM((N, XW), f32)],
        compiler_params=_cparams(2), name="ssd",
    )(proj, proj, proj, proj, proj, cwx, cbx, cwb, cbb, cwc, cbc, dtbias, arow, skip, gain)


def _post_kernel(*refs, emit_next, split_residual):
    if split_residual:
        ctx_ref, lat_ref, *refs = refs
        residual = jnp.where(pl.program_id(1) == 0, ctx_ref[...], lat_ref[...])
    else:
        x_ref, *refs = refs
        residual = x_ref[...]
    gl_ref, ya_ref, yb_ref, yc_ref, yd_ref, m_ref, wbr_ref, wout_ref, wup_ref, wdn_ref, g1_ref, g2_ref, g3_ref, *rest = refs
    D = D_MODEL
    ys = (ya_ref, yb_ref, yc_ref, yd_ref)
    merged = None
    for i in range(N_BRANCH):
        t = _sigmoid(gl_ref[:, D * i:D * (i + 1)]) * _dot(ys[i][...], wbr_ref[i])
        merged = t if merged is None else merged + t
    mix = _dot(merged.astype(bf16), wout_ref[...])
    mod = lambda k: m_ref[:, D * k:D * (k + 1)]
    x1 = residual + mod(2) * _rms(mix, g1_ref[...])
    h2 = (_rms(x1, g2_ref[...]) * (1.0 + mod(4)) + mod(3)).astype(bf16)
    down = None
    hc = 1024
    for c in range(MLP_HIDDEN // hc):
        u = jnp.maximum(_dot(h2, wup_ref[:, hc * c:hc * (c + 1)]), 0.0)
        t = _dot((u * u).astype(bf16), wdn_ref[hc * c:hc * (c + 1), :])
        down = t if down is None else down + t
    x2 = x1 + mod(5) * _rms(down, g3_ref[...])
    if emit_next:
        mn_ref, gn_ref, o_ref, h_ref = rest
        h_ref[...] = _pre_norm(x2, mn_ref, gn_ref)
    else:
        (o_ref,) = rest
    o_ref[...] = x2


def _post(xs, proj, ya, yb, yc, yd, mods_l, wbr, wout, wup, wdn, g1, g2, g3, nxt):
    split = isinstance(xs, tuple)
    B, T, D = proj.shape[0], proj.shape[1], D_MODEL
    W = BRANCH_WIDTH
    nt = T // ROW_TILE
    first = 1 if nxt is None else 0
    assert not (split and first)
    tile = lambda w: pl.BlockSpec((None, ROW_TILE, w), lambda b, i: (b, i + first, 0))
    once = lambda *s: pl.BlockSpec(s, lambda b, i: (0,) * len(s), pipeline_mode=pl.Buffered(1))
    mod_idx = _mod_row_index(B)
    mod_row = pl.BlockSpec((None, 1, 6 * D), lambda b, i: mod_idx(b, i + first))
    out_tile = pl.BlockSpec((None, ROW_TILE, D), lambda b, i: (b, i, 0))
    if split:
        res_specs = [pl.BlockSpec((None, ROW_TILE, D), lambda b, i: (b, 0, 0)),
                     pl.BlockSpec((None, ROW_TILE, D), lambda b, i: (b, jnp.maximum(i - 1, 0), 0))]
        res_args = list(xs)
    else:
        res_specs, res_args = [tile(D)], [xs]
    in_specs = res_specs + [tile(MERGE_COLS), tile(W), tile(W), tile(W), tile(W), mod_row,
                            once(N_BRANCH, W, D), once(D, D), once(D, MLP_HIDDEN), once(MLP_HIDDEN, D),
                            once(1, D), once(1, D), once(1, D)]
    args = res_args + [proj, ya, yb, yc, yd, mods_l, wbr, wout, wup, wdn, g1, g2, g3]
    out_specs, out_shape = out_tile, jax.ShapeDtypeStruct((B, T - first * ROW_TILE, D), f32)
    if nxt is not None:
        in_specs += [mod_row, once(1, D)]
        args += list(nxt)
        out_specs = [out_tile, out_tile]
        out_shape = [out_shape, jax.ShapeDtypeStruct((B, T, D), bf16)]
    return pl.pallas_call(
        functools.partial(_post_kernel, emit_next=nxt is not None, split_residual=split),
        grid=(B, nt - first), in_specs=in_specs, out_specs=out_specs, out_shape=out_shape,
        compiler_params=_cparams(2), name="merge_mlp",
    )(*args)


def _block_diag(w):
    eye = jnp.eye(LRU_BLOCKS, dtype=w.dtype)
    return jnp.einsum('hij,hg->higj', w, eye).reshape(LRU_WIDTH, LRU_WIDTH)


def _rope_tables(n_lat):
    half = ATTN_HEAD_DIM // 2
    quarter = half // 2
    inv_freq = ROPE_BASE ** (-jnp.arange(quarter, dtype=f32) / quarter)
    t = jnp.arange(n_lat, dtype=jnp.int32)
    rows = (t // GRID_W).astype(f32)[:, None] * inv_freq
    cols = (t % GRID_W).astype(f32)[:, None] * inv_freq
    cos_h = jnp.concatenate([jnp.cos(rows), jnp.cos(rows), jnp.cos(cols), jnp.cos(cols)], axis=-1)
    sin_h = jnp.concatenate([-jnp.sin(rows), jnp.sin(rows), -jnp.sin(cols), jnp.sin(cols)], axis=-1)
    return jnp.tile(cos_h, (1, 2)), jnp.tile(sin_h, (1, 2))


def kernel(x, c, ctx, c_ctx, w_ada, b_ada, g_pre_mix, g_post_mix, g_pre_mlp, g_post_mlp, w_in, lru_conv_w, lru_conv_b, lru_rec_w, lru_rec_b, lru_inp_w, lru_inp_b, lru_lambda, attn_sink, hgrn_lb_logits, hgrn_norm_g, ssd_conv_w, ssd_conv_b, ssd_dt_bias, ssd_a_log, ssd_skip, ssd_norm_g, w_branch, w_out, w_mlp_up, w_mlp_down):
    B, L, D = x.shape
    n_ctx = ctx.shape[1]
    assert D == D_MODEL and n_ctx == ROW_TILE == SCAN_CHUNK and L % ROW_TILE == 0 and L % GRID_W == 0
    assert w_in.shape == (DEPTH, D_MODEL, IN_COLS)

    pad_rows = (-(B + 1)) % 8
    c_all = jnp.concatenate([c, c_ctx[None, :], jnp.zeros((pad_rows, D), f32)], axis=0)
    mods = _ada(c_all, w_ada, b_ada)
    mods = mods.reshape(DEPTH, c_all.shape[0], 1, 6 * D)

    lb = jnp.cumsum(jax.nn.softmax(hgrn_lb_logits.astype(f32), axis=0), axis=0)
    lb = lb - lb[0]
    cos_t, sin_t = _rope_tables(L)
    G, R, P, N = SSD_GROUPS, SSD_HPG, SSD_HEAD_DIM, SSD_STATE

    for l in range(DEPTH):
        last = l == DEPTH - 1
        wl = w_in[l].astype(bf16)
        dtf, dtb = wl[:, 5888:5896], wl[:, 5896:5904]
        zpad = jnp.zeros((D, 128 - 2 * R), bf16)
        dt_cols = [jnp.concatenate([dtf[:, R * g:R * (g + 1)], dtb[:, R * g:R * (g + 1)], zpad], axis=1)
                   for g in range(G)]
        w_perm = jnp.concatenate([wl[:, 5904:], wl[:, :5888]] + dt_cols, axis=1)

        if l == 0:
            xs = (ctx, x)
            proj = _first_in_proj(ctx, x, mods[0], g_pre_mix[0][None, :], w_perm)
        else:
            proj = _in_proj(h, w_perm)

        wbd = jnp.concatenate([_block_diag(lru_rec_w[l, 0]), _block_diag(lru_inp_w[l, 0]),
                               _block_diag(lru_rec_w[l, 1]), _block_diag(lru_inp_w[l, 1])], axis=1).astype(bf16)
        gb = jnp.concatenate([lru_rec_b[l, 0], lru_inp_b[l, 0], lru_rec_b[l, 1], lru_inp_b[l, 1]])[None, :]
        csp = LRU_C * jax.nn.softplus(-lru_lambda[l])
        ya = _lru(proj, lru_conv_w[l], lru_conv_b[l][None, :], wbd, gb, csp, n_ctx)

        yb = _attn(proj, attn_sink[l], cos_t, sin_t, n_ctx)

        lbl = lb[l][None, :]
        yc = _hgrn(proj, jnp.log(lbl), jnp.log1p(-lbl), hgrn_norm_g[l][None, :], n_ctx, zero_lb=l == 0)

        cw, cbv = ssd_conv_w[l], ssd_conv_b[l]
        XW = R * P
        grp = lambda a, off, w: jnp.stack([a[..., off + w * g:off + w * (g + 1)] for g in range(G)], axis=0)
        cwx, cbx = grp(cw, 0, XW), grp(cbv[None, :], 0, XW)
        cwb, cbb = grp(cw, SSD_WIDTH, N), grp(cbv[None, :], SSD_WIDTH, N)
        cwc, cbc = grp(cw, SSD_WIDTH + G * N, N), grp(cbv[None, :], SSD_WIDTH + G * N, N)
        a_neg = -jnp.exp(ssd_a_log[l].astype(f32))
        per_group = lambda v: [jnp.concatenate([v[0, R * g:R * (g + 1)], v[1, R * g:R * (g + 1)]]) for g in range(G)]
        lanes_compact = lambda v: jnp.stack([jnp.pad(u, (0, 128 - 2 * R))[None, :] for u in per_group(v)], axis=0)
        lanes_spread = lambda v: jnp.stack([jnp.repeat(u, P)[None, :] for u in per_group(v)], axis=0)
        skip = jnp.repeat(ssd_skip[l], P).reshape(G, 1, XW)
        gain_d = ssd_norm_g[l].reshape(G, 1, XW)
        yd = _ssd(proj, cwx, cbx, cwb, cbb, cwc, cbc, lanes_compact(ssd_dt_bias[l]), lanes_spread(a_neg), skip, gain_d, n_ctx)

        nxt = None if last else (mods[l + 1], g_pre_mix[l + 1][None, :])
        out = _post(xs, proj, ya, yb, yc, yd, mods[l], w_branch[l].astype(bf16), w_out[l].astype(bf16),
                    w_mlp_up[l].astype(bf16), w_mlp_down[l].astype(bf16),
                    g_post_mix[l][None, :], g_pre_mlp[l][None, :], g_post_mlp[l][None, :], nxt)
        if last:
            return out
        xs, h = out
```
